```python
import math
import jax
import jax.numpy as jnp
from jax import lax
import numpy as np

D_MODEL = 2048
BATCH = 1
SEQ = 16384
DEPTH = 2
DEC_BATCH = 8
DEC_SEQ = 4096
PAST_LEN = 128

HEAD_DIM = 128
N_HEADS = D_MODEL // HEAD_DIM
H_NA = N_HEADS // 2
H_DIL = N_HEADS - H_NA
W_NA = H_NA * HEAD_DIM
W_DIL = H_DIL * HEAD_DIM
MIX_W = W_NA + W_DIL
GRID_W = 64
NA_ROWS = 8
NA_COLS = 16
DIL_PAIRS = ((128, 1), (512, 4), (2048, 16))
QB = 128
T5_BUCKETS = 32
T5_MAX_DIST = 2048
D_FF = int(math.ceil(8 * D_MODEL / 3 / 256)) * 256
EPS = 1e-6
NEG = -1e30
SCALE = 1.0 / math.sqrt(HEAD_DIM)

kernel_name = 'hymba_natten_longnet_encoder'


def _rmsnorm(x, g):
    xf = x.astype(jnp.float32)
    y = xf * lax.rsqrt(jnp.mean(xf * xf, axis=-1, keepdims=True) + EPS)
    return (y * g.astype(jnp.float32)).astype(x.dtype)


def _t5_bucket(rel):
    half = T5_BUCKETS // 2
    max_exact = half // 2
    n = np.abs(rel)
    large = max_exact + (np.log(np.maximum(n, max_exact) / max_exact)
                         / np.log(T5_MAX_DIST / max_exact) * (half - max_exact)).astype(np.int32)
    large = np.minimum(large, half - 1)
    return (rel > 0).astype(np.int32) * half + np.where(n < max_exact, n, large).astype(np.int32)


def _band_attention(q, k, v, bias, radius):
    N, L, H, Dh = q.shape
    nb = -(-L // QB)
    pad = nb * QB - L
    kw = QB + 2 * radius
    qp = jnp.pad(q, ((0, 0), (0, pad), (0, 0), (0, 0))).reshape(N, nb, QB, H, Dh)
    kp = jnp.pad(k, ((0, 0), (radius, pad + radius), (0, 0), (0, 0)))
    vp = jnp.pad(v, ((0, 0), (radius, pad + radius), (0, 0), (0, 0)))
    idx = np.arange(nb)[:, None] * QB + np.arange(kw)[None, :]
    ks = kp[:, idx]
    vs = vp[:, idx]
    s = jnp.einsum('nbqhd,nbkhd->nbhqk', qp, ks, preferred_element_type=jnp.float32) * SCALE
    off = np.arange(kw)[None, :] - radius - np.arange(QB)[:, None]
    keypos = idx - radius
    valid = (np.abs(off) <= radius)[None] & ((keypos >= 0) & (keypos < L))[:, None, :]
    b = bias[np.clip(off + radius, 0, 2 * radius)].astype(jnp.float32)
    s = jnp.where(valid[None, :, None], s + jnp.transpose(b, (2, 0, 1))[None, None], NEG)
    m = jnp.max(s, axis=-1, keepdims=True)
    p = jnp.exp(s - m)
    l = jnp.sum(p, axis=-1)
    o = jnp.einsum('nbhqk,nbkhd->nbqhd', p, vs.astype(jnp.float32))
    o = o / jnp.transpose(l, (0, 1, 3, 2))[..., None]
    lse = jnp.transpose(m[..., 0] + jnp.log(l), (0, 1, 3, 2))
    o = o.reshape(N, nb * QB, H, Dh)[:, :L]
    lse = lse.reshape(N, nb * QB, H)[:, :L]
    return o, lse


def _dilated_attention(q, k, v, t5_table):
    B, S, H, Dh = q.shape
    outs, lses = [], []
    for window, dil in DIL_PAIRS:
        radius = window // (2 * dil)
        bias = t5_table[_t5_bucket(dil * np.arange(-radius, radius + 1))]
        def split(a):
            return jnp.transpose(a.reshape(B, S // dil, dil, H, Dh), (0, 2, 1, 3, 4)).reshape(B * dil, S // dil, H, Dh)
        o, lse = _band_attention(split(q), split(k), split(v), bias, radius)
        outs.append(jnp.transpose(o.reshape(B, dil, S // dil, H, Dh), (0, 2, 1, 3, 4)).reshape(B, S, H, Dh))
        lses.append(jnp.transpose(lse.reshape(B, dil, S // dil, H), (0, 2, 1, 3)).reshape(B, S, H))
    wts = jax.nn.softmax(jnp.stack(lses, 0), axis=0)
    return jnp.einsum('pbsh,pbshd->bshd', wts, jnp.stack(outs, 0))


def _neighbourhood_attention(q, k, v, rpb):
    B, L, H, Dh = q.shape
    rows = L // GRID_W
    kh = min(NA_ROWS, rows)
    r = np.arange(rows)
    rs = np.clip(r - kh // 2, 0, rows - kh)
    key_rows = rs[:, None] + np.arange(kh)[None, :]
    c = np.arange(GRID_W)
    cs = np.clip(c - NA_COLS // 2, 0, GRID_W - NA_COLS)
    col_ok = (c[None, :] >= cs[:, None]) & (c[None, :] < cs[:, None] + NA_COLS)
    qg = q.reshape(B, rows, GRID_W, H, Dh)
    kg = k.reshape(B, rows, GRID_W, H, Dh)[:, key_rows]
    vg = v.reshape(B, rows, GRID_W, H, Dh)[:, key_rows]
    s = jnp.einsum('brqhd,brikhd->brhqik', qg, kg, preferred_element_type=jnp.float32) * SCALE
    dr_idx = (key_rows - r[:, None]) + NA_ROWS - 1
    dc_idx = np.clip(c[None, :] - c[:, None] + NA_COLS - 1, 0, 2 * NA_COLS - 2)
    bias = rpb[dr_idx[:, None, :, None], dc_idx[None, :, None, :]]
    bias = jnp.transpose(bias, (0, 4, 1, 2, 3)).astype(jnp.float32)
    s = jnp.where(col_ok[:, None, :], s + bias[None], NEG)
    p = jax.nn.softmax(s.reshape(B, rows, H, GRID_W, kh * GRID_W), axis=-1).reshape(s.shape)
    o = jnp.einsum('brhqik,brikhd->brqhd', p, vg.astype(jnp.float32))
    return o.reshape(B, L, H, Dh)


def _layer(x, w_in, w_out, g_attn, g_na, g_dil, rpb, t5_table, g_ffn, w_gate, w_up, w_down):
    B, L, _ = x.shape
    h = _rmsnorm(x, g_attn)
    proj = h @ w_in
    qa, ka, va, qb, kb, vb = jnp.split(
        proj, [W_NA, 2 * W_NA, 3 * W_NA, 3 * W_NA + W_DIL, 3 * W_NA + 2 * W_DIL], axis=-1)
    heads_a = lambda a: a.reshape(B, L, H_NA, HEAD_DIM)
    heads_b = lambda a: a.reshape(B, L, H_DIL, HEAD_DIM)
    oa = _neighbourhood_attention(heads_a(qa), heads_a(ka), heads_a(va), rpb).reshape(B, L, W_NA).astype(x.dtype)
    ob = _dilated_attention(heads_b(qb), heads_b(kb), heads_b(vb), t5_table).reshape(B, L, W_DIL).astype(x.dtype)
    mix = jnp.concatenate([_rmsnorm(oa, g_na), _rmsnorm(ob, g_dil)], axis=-1)
    x = x + mix @ w_out
    h = _rmsnorm(x, g_ffn)
    return x + (jax.nn.silu(h @ w_gate) * (h @ w_up)) @ w_down


def _trunk(x, w_in, w_out, g_attn, g_na, g_dil, rpb_na, t5_table, g_ffn, w_gate, w_up, w_down, g_final):
    for l in range(DEPTH):
        x = _layer(x, w_in[l], w_out[l], g_attn[l], g_na[l], g_dil[l], rpb_na[l], t5_table,
                   g_ffn[l], w_gate[l], w_up[l], w_down[l])
    return _rmsnorm(x, g_final)


def setup_inputs(seed: int = 0) -> dict:
    key = jax.random.key(seed)
    ks = jax.random.split(key, 16)
    f32 = jnp.float32
    def nrm(k, shape, scale):
        return jax.random.normal(k, shape, f32) * scale
    return {
        'x_prompt': nrm(ks[0], (BATCH, SEQ, D_MODEL), 1.0),
        'x_sample': nrm(ks[1], (DEC_BATCH, DEC_SEQ, D_MODEL), 1.0),
        'w_in': nrm(ks[2], (DEPTH, D_MODEL, 3 * MIX_W), D_MODEL ** -0.5),
        'w_out': nrm(ks[3], (DEPTH, MIX_W, D_MODEL), MIX_W ** -0.5),
        'g_attn': 1.0 + nrm(ks[4], (DEPTH, D_MODEL), 0.02),
        'g_na': 1.0 + nrm(ks[5], (DEPTH, W_NA), 0.02),
        'g_dil': 1.0 + nrm(ks[6], (DEPTH, W_DIL), 0.02),
        'rpb_na': nrm(ks[7], (DEPTH, 2 * NA_ROWS - 1, 2 * NA_COLS - 1, H_NA), 0.1),
        't5_table': nrm(ks[8], (T5_BUCKETS, H_DIL), 0.1),
        'g_ffn': 1.0 + nrm(ks[9], (DEPTH, D_MODEL), 0.02),
        'w_gate': nrm(ks[10], (DEPTH, D_MODEL, D_FF), D_MODEL ** -0.5),
        'w_up': nrm(ks[11], (DEPTH, D_MODEL, D_FF), D_MODEL ** -0.5),
        'w_down': nrm(ks[12], (DEPTH, D_FF, D_MODEL), D_FF ** -0.5),
        'g_final': 1.0 + nrm(ks[13], (D_MODEL,), 0.02),
    }


def reference(x_prompt, x_sample, w_in, w_out, g_attn, g_na, g_dil, rpb_na, t5_table, g_ffn, w_gate, w_up, w_down, g_final):
    y_prompt = _trunk(x_prompt, w_in, w_out, g_attn, g_na, g_dil, rpb_na, t5_table, g_ffn, w_gate, w_up, w_down, g_final)
    y_sample = _trunk(x_sample, w_in, w_out, g_attn, g_na, g_dil, rpb_na, t5_table, g_ffn, w_gate, w_up, w_down, g_final)
    return (y_prompt, y_sample)
```

```python
import functools
import math

import jax
import jax.numpy as jnp
import numpy as np
from jax import lax
from jax.experimental import pallas as pl
from jax.experimental.pallas import tpu as pltpu

HEAD_DIM = 128
H_NA = 8
H_DIL = 8
GRID_W = 64
NA_ROWS = 8
NA_COLS = 16
DIL_PAIRS = ((128, 1), (512, 4), (2048, 16))
T5_BUCKETS = 32
T5_MAX_DIST = 2048
EPS = 1e-6
NEG = -1e30
SCALE = 1.0 / math.sqrt(HEAD_DIM)

RES = 8
RADIUS = 64
TILE_M = 128
NA_SUB_M = 16
NA_WIN_ROWS = 10
NA_WIN_M = NA_WIN_ROWS * GRID_W // RES
VMEM_LIMIT = 56 * 1024 * 1024

BF16 = jnp.bfloat16
F32 = jnp.float32


def _dot_nt(a, b):
    return lax.dot_general(a, b, (((1,), (1,)), ((), ())), preferred_element_type=F32)


def _qkv_kernel(x_ref, g_ref, w_ref, sc_ref, o_ref, h_scr):
    @pl.when(pl.program_id(1) == 0)
    def _():
        x = x_ref[...]
        ms = jnp.mean(x * x, axis=-1, keepdims=True)
        h_scr[...] = (x * lax.rsqrt(ms + EPS) * g_ref[...]).astype(BF16)

    acc = jnp.dot(h_scr[...], w_ref[...], preferred_element_type=F32) * sc_ref[...]
    for c in range(o_ref.shape[0]):
        o_ref[c] = acc[:, c * HEAD_DIM:(c + 1) * HEAD_DIM].astype(BF16)


def _qkv_call(x, g, w, sc, tm, tn):
    R, D = x.shape
    N = w.shape[1]
    return pl.pallas_call(
        _qkv_kernel,
        grid=(R // tm, N // tn),
        in_specs=[
            pl.BlockSpec((tm, D), lambda i, j: (i, 0)),
            pl.BlockSpec((1, D), lambda i, j: (0, 0)),
            pl.BlockSpec((D, tn), lambda i, j: (0, j)),
            pl.BlockSpec((1, tn), lambda i, j: (0, j)),
        ],
        out_specs=pl.BlockSpec((tn // HEAD_DIM, tm, HEAD_DIM), lambda i, j: (j, i, 0)),
        out_shape=jax.ShapeDtypeStruct((N // HEAD_DIM, R, HEAD_DIM), BF16),
        scratch_shapes=[pltpu.VMEM((tm, D), BF16)],
        compiler_params=pltpu.CompilerParams(
            dimension_semantics=("parallel", "arbitrary"), vmem_limit_bytes=VMEM_LIMIT),
        name="qkv_proj",
    )(x, g, w, sc)


def _group_norm(ref, g):
    a = jnp.concatenate([ref[c] for c in range(ref.shape[0])], axis=1).astype(F32)
    ms = jnp.mean(a * a, axis=-1, keepdims=True)
    return (a * lax.rsqrt(ms + EPS) * g).astype(BF16)


def _outproj_kernel(oa_ref, ob_ref, ga_ref, gb_ref, w_ref, x_ref, o_ref):
    mix = jnp.concatenate([_group_norm(oa_ref, ga_ref[...]), _group_norm(ob_ref, gb_ref[...])], axis=1)
    o_ref[...] = x_ref[...] + jnp.dot(mix, w_ref[...], preferred_element_type=F32)


def _outproj_call(oa, ob, ga, gb, w, x, tm):
    R, D = x.shape
    ha, hb = oa.shape[0], ob.shape[0]
    return pl.pallas_call(
        _outproj_kernel,
        grid=(R // tm,),
        in_specs=[
            pl.BlockSpec((ha, tm, HEAD_DIM), lambda i: (0, i, 0)),
            pl.BlockSpec((hb, tm, HEAD_DIM), lambda i: (0, i, 0)),
            pl.BlockSpec((1, ha * HEAD_DIM), lambda i: (0, 0)),
            pl.BlockSpec((1, hb * HEAD_DIM), lambda i: (0, 0)),
            pl.BlockSpec(w.shape, lambda i: (0, 0)),
            pl.BlockSpec((tm, D), lambda i: (i, 0)),
        ],
        out_specs=pl.BlockSpec((tm, D), lambda i: (i, 0)),
        out_shape=jax.ShapeDtypeStruct((R, D), F32),
        compiler_params=pltpu.CompilerParams(
            dimension_semantics=("parallel",), vmem_limit_bytes=VMEM_LIMIT),
        name="out_proj",
    )(oa, ob, ga, gb, w, x)


def _ffn_kernel(x_ref, g_ref, wg_ref, wu_ref, wd_ref, gf_ref, o_ref, h_scr, *, final_norm):
    f = pl.program_id(1)

    @pl.when(f == 0)
    def _():
        x = x_ref[...]
        ms = jnp.mean(x * x, axis=-1, keepdims=True)
        h_scr[...] = (x * lax.rsqrt(ms + EPS) * g_ref[...]).astype(BF16)
        o_ref[...] = x

    h = h_scr[...]
    gate = jnp.dot(h, wg_ref[...], preferred_element_type=F32)
    up = jnp.dot(h, wu_ref[...], preferred_element_type=F32)
    act = (gate * (1.0 / (1.0 + jnp.exp(-gate))) * up).astype(BF16)
    o_ref[...] += jnp.dot(act, wd_ref[...], preferred_element_type=F32)

    if final_norm:
        @pl.when(f == pl.num_programs(1) - 1)
        def _():
            y = o_ref[...]
            ms = jnp.mean(y * y, axis=-1, keepdims=True)
            o_ref[...] = y * lax.rsqrt(ms + EPS) * gf_ref[...]


def _ffn_call(x, g, wg, wu, wd, gf, tm, tf, final_norm):
    R, D = x.shape
    F = wg.shape[1]
    return pl.pallas_call(
        functools.partial(_ffn_kernel, final_norm=final_norm),
        grid=(R // tm, F // tf),
        in_specs=[
            pl.BlockSpec((tm, D), lambda i, f: (i, 0)),
            pl.BlockSpec((1, D), lambda i, f: (0, 0)),
            pl.BlockSpec((D, tf), lambda i, f: (0, f)),
            pl.BlockSpec((D, tf), lambda i, f: (0, f)),
            pl.BlockSpec((tf, D), lambda i, f: (f, 0)),
            pl.BlockSpec((1, D), lambda i, f: (0, 0)),
        ],
        out_specs=pl.BlockSpec((tm, D), lambda i, f: (i, 0)),
        out_shape=jax.ShapeDtypeStruct((R, D), F32),
        scratch_shapes=[pltpu.VMEM((tm, D), BF16)],
        compiler_params=pltpu.CompilerParams(
            dimension_semantics=("parallel", "arbitrary"), vmem_limit_bytes=VMEM_LIMIT),
        name="ffn",
    )(x, g, wg, wu, wd, gf)


def _t5_bucket(rel):
    half = T5_BUCKETS // 2
    max_exact = half // 2
    n = np.abs(rel)
    large = max_exact + (np.log(np.maximum(n, max_exact) / max_exact)
                         / np.log(T5_MAX_DIST / max_exact) * (half - max_exact)).astype(np.int32)
    large = np.minimum(large, half - 1)
    return (rel > 0).astype(np.int32) * half + np.where(n < max_exact, n, large).astype(np.int32)


def _toeplitz_table(bias, rel, ok):
    idx = np.clip(rel + RADIUS, 0, 2 * RADIUS)
    t = jnp.where(ok[..., None], bias[idx], NEG)
    return jnp.transpose(t, (3, 0, 1, 2))


def _dil_geometry(q_m, k_m):
    geo = {}
    qm, km = 128, 384
    halo = (km - qm) // 2
    rel, ok = [], []
    for off in (-halo, 0, -2 * halo):
        d = off + np.arange(km)[None, :] - np.arange(qm)[:, None]
        rel.append(d // 2)
        ok.append((d % 2 == 0) & (np.abs(d // 2) <= RADIUS))
    geo[16] = (qm, km, np.stack(rel), np.stack(ok))
    qm, km = 64, 128
    halo = (km - qm) // 2
    rel, ok = [], []
    for off in (-halo, 0, -2 * halo):
        qa, qml = np.divmod(np.arange(2 * qm), qm)
        ka, kml = np.divmod(np.arange(2 * km), km)
        j = 2 * (off + kml[None, :] - qml[:, None]) + ka[None, :] - qa[:, None]
        rel.append(j)
        ok.append(np.abs(j) <= RADIUS)
    geo[4] = (qm, km, np.stack(rel), np.stack(ok))
    qm, km = 32, 64
    halo = (km - qm) // 2
    rel, ok = [], []
    for off in (-halo, 0, -2 * halo):
        qr, qml = np.divmod(np.arange(RES * qm), qm)
        kr, kml = np.divmod(np.arange(RES * km), km)
        j = RES * (off + kml[None, :] - qml[:, None]) + kr[None, :] - qr[:, None]
        rel.append(j)
        ok.append(np.abs(j) <= RADIUS)
    geo[1] = (qm, km, np.stack(rel), np.stack(ok))
    return geo


_DIL_GEO = _dil_geometry(None, None)


def _dil_tables(t5_table):
    out = []
    for _, dil in DIL_PAIRS:
        bias = t5_table[_t5_bucket(dil * np.arange(-RADIUS, RADIUS + 1))].astype(F32)
        _, _, rel, ok = _DIL_GEO[dil]
        out.append(_toeplitz_table(bias, rel, ok))
    return out


def _na_table(rpb, rows):
    npairs = rows // 2
    c = np.arange(GRID_W)
    cs = np.clip(c - NA_COLS // 2, 0, GRID_W - NA_COLS)
    tabs = []
    for r2 in (2, 0, 1, npairs - 2, npairs - 1):
        w0 = int(np.clip(2 * r2 - 4, 0, rows - NA_WIN_ROWS))
        qrho, qml = np.divmod(np.arange(RES * NA_SUB_M), NA_SUB_M)
        krho, kml = np.divmod(np.arange(RES * NA_WIN_M), NA_WIN_M)
        rq = 2 * r2 + qml // 8
        cq = 8 * (qml % 8) + qrho
        rk = w0 + kml // 8
        ck = 8 * (kml % 8) + krho
        rs = np.clip(rq - NA_ROWS // 2, 0, rows - NA_ROWS)
        row_ok = (rk[None, :] >= rs[:, None]) & (rk[None, :] < rs[:, None] + NA_ROWS)
        col_ok = (ck[None, :] >= cs[cq][:, None]) & (ck[None, :] < cs[cq][:, None] + NA_COLS)
        dr = np.clip(rk[None, :] - rq[:, None] + NA_ROWS - 1, 0, 2 * NA_ROWS - 2)
        dc = np.clip(ck[None, :] - cq[:, None] + NA_COLS - 1, 0, 2 * NA_COLS - 2)
        tabs.append(jnp.where((row_ok & col_ok)[..., None], rpb[dr, dc].astype(F32), NEG))
    return jnp.transpose(jnp.stack(tabs), (3, 0, 1, 2))


def _na_kernel(q_ref, k_ref, v_ref, t_ref, o_ref, *, rows):
    npairs = rows // 2
    subs = TILE_M // NA_SUB_M
    tile = pl.program_id(2)

    def body(j, carry):
        r2 = tile * subs + j
        w0 = jnp.clip(2 * r2 - 4, 0, rows - NA_WIN_ROWS)
        var = jnp.where(r2 == 0, 1, jnp.where(r2 == 1, 2, jnp.where(
            r2 == npairs - 2, 3, jnp.where(r2 == npairs - 1, 4, 0))))
        ms = pl.multiple_of(j * NA_SUB_M, NA_SUB_M)
        ws = pl.multiple_of(w0 * (GRID_W // RES), 16)
        q = jnp.concatenate([q_ref[r, pl.ds(ms, NA_SUB_M), :] for r in range(RES)], axis=0)
        k = jnp.concatenate([k_ref[r, pl.ds(ws, NA_WIN_M), :] for r in range(RES)], axis=0)
        v = jnp.concatenate([v_ref[r, pl.ds(ws, NA_WIN_M), :] for r in range(RES)], axis=0)
        s = _dot_nt(q, k) + t_ref[var]
        m = jnp.max(s, axis=-1, keepdims=True)
        p = jnp.exp(s - m)
        l = jnp.sum(p, axis=-1, keepdims=True)
        o = jnp.dot(p.astype(BF16), v, preferred_element_type=F32) / l
        for r in range(RES):
            o_ref[r, pl.ds(ms, NA_SUB_M), :] = o[r * NA_SUB_M:(r + 1) * NA_SUB_M].astype(BF16)
        return carry

    lax.fori_loop(0, subs, body, 0)


def _na_call(qkv, table, rows):
    _, B, _, M, _ = qkv.shape
    blk_q = (None, None, RES, TILE_M, HEAD_DIM)
    blk_kv = (None, None, RES, M, HEAD_DIM)
    return pl.pallas_call(
        functools.partial(_na_kernel, rows=rows),
        grid=(B, H_NA, M // TILE_M),
        in_specs=[
            pl.BlockSpec(blk_q, lambda b, h, i: (h, b, 0, i, 0)),
            pl.BlockSpec(blk_kv, lambda b, h, i: (H_NA + h, b, 0, 0, 0)),
            pl.BlockSpec(blk_kv, lambda b, h, i: (2 * H_NA + h, b, 0, 0, 0)),
            pl.BlockSpec((None,) + table.shape[1:], lambda b, h, i: (h, 0, 0, 0)),
        ],
        out_specs=pl.BlockSpec(blk_q, lambda b, h, i: (h, b, 0, i, 0)),
        out_shape=jax.ShapeDtypeStruct((H_NA, B, RES, M, HEAD_DIM), BF16),
        compiler_params=pltpu.CompilerParams(
            dimension_semantics=("parallel", "parallel", "arbitrary"), vmem_limit_bytes=VMEM_LIMIT),
        name="na_attn",
    )(qkv, qkv, qkv, table)


def _edge_variant(blk, nblk):
    return jnp.where(blk == 0, 1, jnp.where(blk == nblk - 1, 2, 0))


def _flash_step(q, k, v, bias, old):
    s = _dot_nt(q, k) + bias
    m_blk = jnp.max(s, axis=-1, keepdims=True)
    if old is None:
        m_new = m_blk
    else:
        m_old, l_old, acc_old = old
        m_new = jnp.maximum(m_old[:, :1], m_blk)
        alpha = jnp.exp(m_old - m_new)
    p = jnp.exp(s - m_new)
    l_new = jnp.sum(p, axis=-1, keepdims=True)
    acc = jnp.dot(p.astype(BF16), v, preferred_element_type=F32)
    if old is not None:
        l_new = alpha * l_old + l_new
        acc = alpha * acc_old + acc
    wide = (q.shape[0], HEAD_DIM)
    return jnp.broadcast_to(m_new, wide), jnp.broadcast_to(l_new, wide), acc


def _dil_kernel(q_ref, k_ref, v_ref, t1_ref, t4_ref, t16_ref, o_ref, m_scr, l_scr, acc_scr, *, m_total):
    tile = pl.program_id(2)
    m_base = tile * TILE_M

    qm, km = _DIL_GEO[16][:2]
    var = _edge_variant(tile, m_total // qm)
    ws = pl.multiple_of(jnp.clip(m_base - (km - qm) // 2, 0, m_total - km), 16)

    def body16(r, carry):
        m, l, acc = _flash_step(q_ref[r], k_ref[r, pl.ds(ws, km), :], v_ref[r, pl.ds(ws, km), :],
                                t16_ref[var], None)
        m_scr[r] = m
        l_scr[r] = l
        acc_scr[r] = acc
        return carry

    lax.fori_loop(0, RES, body16, 0)

    qm, km = _DIL_GEO[4][:2]
    nsub = TILE_M // qm

    def body4(n, carry):
        r4 = n // nsub
        sub = n % nsub
        ms = pl.multiple_of(sub * qm, qm)
        blk = tile * nsub + sub
        var = _edge_variant(blk, m_total // qm)
        ws = pl.multiple_of(jnp.clip(blk * qm - (km - qm) // 2, 0, m_total - km), 16)
        rr = (r4, r4 + 4)
        q = jnp.concatenate([q_ref[r, pl.ds(ms, qm), :] for r in rr], axis=0)
        k = jnp.concatenate([k_ref[r, pl.ds(ws, km), :] for r in rr], axis=0)
        v = jnp.concatenate([v_ref[r, pl.ds(ws, km), :] for r in rr], axis=0)
        old = tuple(jnp.concatenate([ref[r, pl.ds(ms, qm), :] for r in rr], axis=0)
                    for ref in (m_scr, l_scr, acc_scr))
        new = _flash_step(q, k, v, t4_ref[var], old)
        for ref, val in zip((m_scr, l_scr, acc_scr), new):
            for a, r in enumerate(rr):
                ref[r, pl.ds(ms, qm), :] = val[a * qm:(a + 1) * qm]
        return carry

    lax.fori_loop(0, 4 * nsub, body4, 0)

    qm, km = _DIL_GEO[1][:2]
    nsub = TILE_M // qm

    def body1(sub, carry):
        ms = pl.multiple_of(sub * qm, qm)
        blk = tile * nsub + sub
        var = _edge_variant(blk, m_total // qm)
        ws = pl.multiple_of(jnp.clip(blk * qm - (km - qm) // 2, 0, m_total - km), 16)
        q = jnp.concatenate([q_ref[r, pl.ds(ms, qm), :] for r in range(RES)], axis=0)
        k = jnp.concatenate([k_ref[r, pl.ds(ws, km), :] for r in range(RES)], axis=0)
        v = jnp.concatenate([v_ref[r, pl.ds(ws, km), :] for r in range(RES)], axis=0)
        old = tuple(jnp.concatenate([ref[r, pl.ds(ms, qm), :] for r in range(RES)], axis=0)
                    for ref in (m_scr, l_scr, acc_scr))
        new = _flash_step(q, k, v, t1_ref[var], old)
        for ref, val in zip((m_scr, l_scr, acc_scr), new):
            for r in range(RES):
                ref[r, pl.ds(ms, qm), :] = val[r * qm:(r + 1) * qm]
        return carry

    lax.fori_loop(0, nsub, body1, 0)

    o_ref[...] = (acc_scr[...] / l_scr[...]).astype(BF16)


def _dil_call(qkv, t1, t4, t16):
    _, B, _, M, _ = qkv.shape
    base = 3 * H_NA
    blk_q = (None, None, RES, TILE_M, HEAD_DIM)
    blk_kv = (None, None, RES, M, HEAD_DIM)
    tspec = lambda t: pl.BlockSpec((None,) + t.shape[1:], lambda b, h, i: (h, 0, 0, 0))
    stat = pltpu.VMEM((RES, TILE_M, HEAD_DIM), F32)
    return pl.pallas_call(
        functools.partial(_dil_kernel, m_total=M),
        grid=(B, H_DIL, M // TILE_M),
        in_specs=[
            pl.BlockSpec(blk_q, lambda b, h, i: (base + h, b, 0, i, 0)),
            pl.BlockSpec(blk_kv, lambda b, h, i: (base + H_DIL + h, b, 0, 0, 0)),
            pl.BlockSpec(blk_kv, lambda b, h, i: (base + 2 * H_DIL + h, b, 0, 0, 0)),
            tspec(t1), tspec(t4), tspec(t16),
        ],
        out_specs=pl.BlockSpec(blk_q, lambda b, h, i: (h, b, 0, i, 0)),
        out_shape=jax.ShapeDtypeStruct((H_DIL, B, RES, M, HEAD_DIM), BF16),
        scratch_shapes=[stat, stat, stat],
        compiler_params=pltpu.CompilerParams(
            dimension_semantics=("parallel", "parallel", "arbitrary"), vmem_limit_bytes=VMEM_LIMIT),
        name="dil_attn",
    )(qkv, qkv, qkv, t1, t4, t16)


def _trunk(x, w_in, w_out, g_attn, g_na, g_dil, na_tables, dil_tables, g_ffn, w_gate, w_up, w_down,
           g_final, q_scale):
    B, S, D = x.shape
    M = S // RES
    R = B * S
    depth = w_in.shape[0]
    assert S % (RES * TILE_M) == 0 and M >= 512 and S % (2 * GRID_W) == 0
    rows = S // GRID_W
    x8 = jnp.transpose(x.reshape(B, M, RES, D), (0, 2, 1, 3)).reshape(R, D)
    for l in range(depth):
        qkv = _qkv_call(x8, g_attn[l][None], w_in[l], q_scale, tm=1024, tn=1024)
        qkv = qkv.reshape(qkv.shape[0], B, RES, M, HEAD_DIM)
        oa = _na_call(qkv, na_tables[(l, rows)], rows).reshape(H_NA, R, HEAD_DIM)
        ob = _dil_call(qkv, *dil_tables).reshape(H_DIL, R, HEAD_DIM)
        x1 = _outproj_call(oa, ob, g_na[l][None], g_dil[l][None], w_out[l], x8, tm=512)
        x8 = _ffn_call(x1, g_ffn[l][None], w_gate[l], w_up[l], w_down[l], g_final[None],
                       tm=512, tf=512, final_norm=(l == depth - 1))
    return jnp.transpose(x8.reshape(B, RES, M, D), (0, 2, 1, 3)).reshape(B, S, D)


def kernel(x_prompt, x_sample, w_in, w_out, g_attn, g_na, g_dil, rpb_na, t5_table, g_ffn, w_gate, w_up,
           w_down, g_final):
    depth = w_in.shape[0]
    w_na = H_NA * HEAD_DIM
    w_dil = H_DIL * HEAD_DIM
    col = np.arange(3 * (w_na + w_dil))
    is_q = (col < w_na) | ((col >= 3 * w_na) & (col < 3 * w_na + w_dil))
    q_scale = jnp.asarray(np.where(is_q, SCALE, 1.0)[None], F32)
    dil_tables = _dil_tables(t5_table)
    na_tables = {}
    for x in (x_prompt, x_sample):
        rows = x.shape[1] // GRID_W
        for l in range(depth):
            if (l, rows) not in na_tables:
                na_tables[(l, rows)] = _na_table(rpb_na[l], rows)
    wb = [w.astype(BF16) for w in (w_in, w_out, w_gate, w_up, w_down)]
    outs = []
    for x in (x_prompt, x_sample):
        outs.append(_trunk(x, wb[0], wb[1], g_attn, g_na, g_dil, na_tables, dil_tables, g_ffn,
                           wb[2], wb[3], wb[4], g_final, q_scale))
    return tuple(outs)
```

```python
import functools
import math

import jax
import jax.numpy as jnp
import numpy as np
from jax import lax
from jax.experimental import pallas as pl
from jax.experimental.pallas import tpu as pltpu

HEAD_DIM = 128
H_NA = 8
H_DIL = 8
GRID_W = 64
NA_ROWS = 8
NA_COLS = 16
DIL_PAIRS = ((128, 1), (512, 4), (2048, 16))
T5_BUCKETS = 32
T5_MAX_DIST = 2048
EPS = 1e-6
NEG = -1e30
SCALE = 1.0 / math.sqrt(HEAD_DIM)

RES = 8
RADIUS = 64
TILE_M = 128
NA_SUB_M = 16
NA_WIN_ROWS = 10
NA_WIN_M = NA_WIN_ROWS * GRID_W // RES
VMEM_LIMIT = 56 * 1024 * 1024

BF16 = jnp.bfloat16
F32 = jnp.float32


def _dot_nt(a, b):
    return lax.dot_general(a, b, (((1,), (1,)), ((), ())), preferred_element_type=F32)


def _qkv_kernel(x_ref, g_ref, w_ref, sc_ref, o_ref, h_scr):
    @pl.when(pl.program_id(1) == 0)
    def _():
        x = x_ref[...]
        ms = jnp.mean(x * x, axis=-1, keepdims=True)
        h_scr[...] = (x * lax.rsqrt(ms + EPS) * g_ref[...]).astype(BF16)

    acc = jnp.dot(h_scr[...], w_ref[...], preferred_element_type=F32) * sc_ref[...]
    for c in range(o_ref.shape[0]):
        o_ref[c] = acc[:, c * HEAD_DIM:(c + 1) * HEAD_DIM].astype(BF16)


def _qkv_call(x, g, w, sc, tm, tn):
    R, D = x.shape
    N = w.shape[1]
    return pl.pallas_call(
        _qkv_kernel,
        grid=(R // tm, N // tn),
        in_specs=[
            pl.BlockSpec((tm, D), lambda i, j: (i, 0)),
            pl.BlockSpec((1, D), lambda i, j: (0, 0)),
            pl.BlockSpec((D, tn), lambda i, j: (0, j)),
            pl.BlockSpec((1, tn), lambda i, j: (0, j)),
        ],
        out_specs=pl.BlockSpec((tn // HEAD_DIM, tm, HEAD_DIM), lambda i, j: (j, i, 0)),
        out_shape=jax.ShapeDtypeStruct((N // HEAD_DIM, R, HEAD_DIM), BF16),
        scratch_shapes=[pltpu.VMEM((tm, D), BF16)],
        compiler_params=pltpu.CompilerParams(
            dimension_semantics=("parallel", "arbitrary"), vmem_limit_bytes=VMEM_LIMIT),
        name="qkv_proj",
    )(x, g, w, sc)


def _group_norm(ref, g):
    a = jnp.concatenate([ref[c] for c in range(ref.shape[0])], axis=1).astype(F32)
    ms = jnp.mean(a * a, axis=-1, keepdims=True)
    return (a * lax.rsqrt(ms + EPS) * g).astype(BF16)


def _outproj_kernel(oa_ref, ob_ref, ga_ref, gb_ref, w_ref, x_ref, o_ref):
    mix = jnp.concatenate([_group_norm(oa_ref, ga_ref[...]), _group_norm(ob_ref, gb_ref[...])], axis=1)
    o_ref[...] = x_ref[...] + jnp.dot(mix, w_ref[...], preferred_element_type=F32)


def _outproj_call(oa, ob, ga, gb, w, x, tm):
    R, D = x.shape
    ha, hb = oa.shape[0], ob.shape[0]
    return pl.pallas_call(
        _outproj_kernel,
        grid=(R // tm,),
        in_specs=[
            pl.BlockSpec((ha, tm, HEAD_DIM), lambda i: (0, i, 0)),
            pl.BlockSpec((hb, tm, HEAD_DIM), lambda i: (0, i, 0)),
            pl.BlockSpec((1, ha * HEAD_DIM), lambda i: (0, 0)),
            pl.BlockSpec((1, hb * HEAD_DIM), lambda i: (0, 0)),
            pl.BlockSpec(w.shape, lambda i: (0, 0)),
            pl.BlockSpec((tm, D), lambda i: (i, 0)),
        ],
        out_specs=pl.BlockSpec((tm, D), lambda i: (i, 0)),
        out_shape=jax.ShapeDtypeStruct((R, D), F32),
        compiler_params=pltpu.CompilerParams(
            dimension_semantics=("parallel",), vmem_limit_bytes=VMEM_LIMIT),
        name="out_proj",
    )(oa, ob, ga, gb, w, x)


def _ffn_kernel(x_ref, g_ref, wg_ref, wu_ref, wd_ref, gf_ref, o_ref, h_scr, *, final_norm):
    f = pl.program_id(1)

    @pl.when(f == 0)
    def _():
        x = x_ref[...]
        ms = jnp.mean(x * x, axis=-1, keepdims=True)
        h_scr[...] = (x * lax.rsqrt(ms + EPS) * g_ref[...]).astype(BF16)
        o_ref[...] = x

    h = h_scr[...]
    gate = jnp.dot(h, wg_ref[...], preferred_element_type=F32)
    up = jnp.dot(h, wu_ref[...], preferred_element_type=F32)
    act = (gate * (1.0 / (1.0 + jnp.exp(-gate))) * up).astype(BF16)
    o_ref[...] += jnp.dot(act, wd_ref[...], preferred_element_type=F32)

    if final_norm:
        @pl.when(f == pl.num_programs(1) - 1)
        def _():
            y = o_ref[...]
            ms = jnp.mean(y * y, axis=-1, keepdims=True)
            o_ref[...] = y * lax.rsqrt(ms + EPS) * gf_ref[...]


def _ffn_call(x, g, wg, wu, wd, gf, tm, tf, final_norm):
    R, D = x.shape
    F = wg.shape[1]
    return pl.pallas_call(
        functools.partial(_ffn_kernel, final_norm=final_norm),
        grid=(R // tm, F // tf),
        in_specs=[
            pl.BlockSpec((tm, D), lambda i, f: (i, 0)),
            pl.BlockSpec((1, D), lambda i, f: (0, 0)),
            pl.BlockSpec((D, tf), lambda i, f: (0, f)),
            pl.BlockSpec((D, tf), lambda i, f: (0, f)),
            pl.BlockSpec((tf, D), lambda i, f: (f, 0)),
            pl.BlockSpec((1, D), lambda i, f: (0, 0)),
        ],
        out_specs=pl.BlockSpec((tm, D), lambda i, f: (i, 0)),
        out_shape=jax.ShapeDtypeStruct((R, D), F32),
        scratch_shapes=[pltpu.VMEM((tm, D), BF16)],
        compiler_params=pltpu.CompilerParams(
            dimension_semantics=("parallel", "arbitrary"), vmem_limit_bytes=VMEM_LIMIT),
        name="ffn",
    )(x, g, wg, wu, wd, gf)


def _t5_bucket(rel):
    half = T5_BUCKETS // 2
    max_exact = half // 2
    n = np.abs(rel)
    large = max_exact + (np.log(np.maximum(n, max_exact) / max_exact)
                         / np.log(T5_MAX_DIST / max_exact) * (half - max_exact)).astype(np.int32)
    large = np.minimum(large, half - 1)
    return (rel > 0).astype(np.int32) * half + np.where(n < max_exact, n, large).astype(np.int32)


def _toeplitz(e, nq):
    length = e.shape[-1]
    f = jnp.concatenate([e, jnp.zeros(e.shape[:-1] + (1,), e.dtype)], axis=-1)
    flat = jnp.tile(f, (1,) * (e.ndim - 1) + (nq,))[..., :nq * length]
    return flat.reshape(e.shape[:-1] + (nq, length))[..., nq - 1:]


_DIL_GEO = {16: (128, 384, 1, 1), 4: (64, 128, 2, 2), 1: (32, 64, RES, RES)}
_EXT = 2048


def _dil_table(bias, dil):
    qm, km, nres, stride = _DIL_GEO[dil]
    heads = bias.shape[0]
    if dil == 16:
        band = jnp.stack([bias, jnp.full_like(bias, NEG)], axis=-1).reshape(heads, -1)[:, :4 * RADIUS + 1]
    else:
        band = bias
    half = band.shape[-1] // 2
    neg = jnp.full((heads, _EXT - half), NEG, F32)
    ext = jnp.concatenate([neg, band, neg], axis=-1)
    nq, nk = qm * stride, km * stride
    halo = (km - qm) // 2
    tabs = []
    for off in (-halo, 0, -2 * halo):
        lo = stride * off - (nq - 1) + _EXT
        t = _toeplitz(ext[:, lo:lo + nq + nk - 1], nq)
        t = t.reshape(heads, qm, nres, km, nres)
        tabs.append(jnp.transpose(t, (0, 2, 1, 4, 3)).reshape(heads, nq, nk))
    return jnp.stack(tabs, axis=1)


def _dil_tables(t5_table):
    out = []
    for _, dil in DIL_PAIRS:
        bias = t5_table[_t5_bucket(dil * np.arange(-RADIUS, RADIUS + 1))].astype(F32)
        out.append(_dil_table(bias.T, dil))
    return out


def _na_table(rpb, rows):
    npairs = rows // 2
    heads = rpb.shape[-1]
    c = np.arange(GRID_W)
    cs = np.clip(c - NA_COLS // 2, 0, GRID_W - NA_COLS)
    col_ok = (c[None, :] >= cs[:, None]) & (c[None, :] < cs[:, None] + NA_COLS)
    pad = GRID_W - NA_COLS
    e = jnp.pad(jnp.transpose(rpb, (2, 0, 1)).astype(F32), ((0, 0), (0, 0), (pad, pad)))
    cmat = jnp.where(col_ok, _toeplitz(e, GRID_W), NEG)
    dr_idx, row_ok = [], []
    for r2 in (2, 0, 1, npairs - 2, npairs - 1):
        w0 = int(np.clip(2 * r2 - 4, 0, rows - NA_WIN_ROWS))
        rq = 2 * r2 + np.arange(2)
        rk = w0 + np.arange(NA_WIN_ROWS)
        rs = np.clip(rq - NA_ROWS // 2, 0, rows - NA_ROWS)
        row_ok.append((rk[None, :] >= rs[:, None]) & (rk[None, :] < rs[:, None] + NA_ROWS))
        dr_idx.append(np.clip(rk[None, :] - rq[:, None] + NA_ROWS - 1, 0, 2 * NA_ROWS - 2))
    g = jnp.take(cmat, np.stack(dr_idx), axis=1)
    g = jnp.where(np.stack(row_ok)[..., None, None], g, NEG)
    sub = GRID_W // RES
    g = g.reshape(heads, 5, 2, NA_WIN_ROWS, sub, RES, sub, RES)
    g = jnp.transpose(g, (0, 1, 5, 2, 4, 7, 3, 6))
    return g.reshape(heads, 5, RES * NA_SUB_M, RES * NA_WIN_M)


def _na_kernel(q_ref, k_ref, v_ref, t_ref, o_ref, *, rows):
    npairs = rows // 2
    subs = TILE_M // NA_SUB_M
    tile = pl.program_id(2)

    for j in range(subs):
        r2 = tile * subs + j
        w0 = jnp.clip(2 * r2 - 4, 0, rows - NA_WIN_ROWS)
        var = jnp.where(r2 == 0, 1, jnp.where(r2 == 1, 2, jnp.where(
            r2 == npairs - 2, 3, jnp.where(r2 == npairs - 1, 4, 0))))
        ms = j * NA_SUB_M
        ws = pl.multiple_of(w0 * (GRID_W // RES), 16)
        q = jnp.concatenate([q_ref[r, pl.ds(ms, NA_SUB_M), :] for r in range(RES)], axis=0)
        k = jnp.concatenate([k_ref[r, pl.ds(ws, NA_WIN_M), :] for r in range(RES)], axis=0)
        v = jnp.concatenate([v_ref[r, pl.ds(ws, NA_WIN_M), :] for r in range(RES)], axis=0)
        s = _dot_nt(q, k) + t_ref[var]
        m = jnp.max(s, axis=-1, keepdims=True)
        p = jnp.exp(s - m)
        l = jnp.sum(p, axis=-1, keepdims=True)
        o = jnp.dot(p.astype(BF16), v, preferred_element_type=F32) / l
        for r in range(RES):
            o_ref[r, pl.ds(ms, NA_SUB_M), :] = o[r * NA_SUB_M:(r + 1) * NA_SUB_M].astype(BF16)


def _na_call(qkv, table, rows):
    _, B, _, M, _ = qkv.shape
    blk_q = (None, None, RES, TILE_M, HEAD_DIM)
    blk_kv = (None, None, RES, M, HEAD_DIM)
    return pl.pallas_call(
        functools.partial(_na_kernel, rows=rows),
        grid=(B, H_NA, M // TILE_M),
        in_specs=[
            pl.BlockSpec(blk_q, lambda b, h, i: (h, b, 0, i, 0)),
            pl.BlockSpec(blk_kv, lambda b, h, i: (H_NA + h, b, 0, 0, 0)),
            pl.BlockSpec(blk_kv, lambda b, h, i: (2 * H_NA + h, b, 0, 0, 0)),
            pl.BlockSpec((None,) + table.shape[1:], lambda b, h, i: (h, 0, 0, 0)),
        ],
        out_specs=pl.BlockSpec(blk_q, lambda b, h, i: (h, b, 0, i, 0)),
        out_shape=jax.ShapeDtypeStruct((H_NA, B, RES, M, HEAD_DIM), BF16),
        compiler_params=pltpu.CompilerParams(
            dimension_semantics=("parallel", "parallel", "arbitrary"), vmem_limit_bytes=VMEM_LIMIT),
        name="na_attn",
    )(qkv, qkv, qkv, table)


def _edge_variant(blk, nblk):
    return jnp.where(blk == 0, 1, jnp.where(blk == nblk - 1, 2, 0))


def _flash_step(q, k, v, bias, old):
    s = _dot_nt(q, k) + bias
    m_blk = jnp.max(s, axis=-1, keepdims=True)
    if old is None:
        m_new = m_blk
    else:
        m_old, l_old, acc_old = old
        m_new = jnp.maximum(m_old[:, :1], m_blk)
        alpha = jnp.exp(m_old - m_new)
    p = jnp.exp(s - m_new)
    l_new = jnp.sum(p, axis=-1, keepdims=True)
    acc = jnp.dot(p.astype(BF16), v, preferred_element_type=F32)
    if old is not None:
        l_new = alpha * l_old + l_new
        acc = alpha * acc_old + acc
    wide = (q.shape[0], HEAD_DIM)
    return jnp.broadcast_to(m_new, wide), jnp.broadcast_to(l_new, wide), acc


def _dil_kernel(q_ref, k_ref, v_ref, t1_ref, t4_ref, t16_ref, o_ref, m_scr, l_scr, acc_scr, *, m_total):
    tile = pl.program_id(2)
    stats = (m_scr, l_scr, acc_scr)

    qm, km = _DIL_GEO[16][:2]
    var = _edge_variant(tile, m_total // qm)
    ws = pl.multiple_of(jnp.clip(tile * qm - (km - qm) // 2, 0, m_total - km), 16)
    for r in range(RES):
        new = _flash_step(q_ref[r], k_ref[r, pl.ds(ws, km), :], v_ref[r, pl.ds(ws, km), :],
                          t16_ref[var], None)
        for ref, val in zip(stats, new):
            ref[r] = val

    for dil, t_ref in ((4, t4_ref), (1, t1_ref)):
        qm, km, nres, _ = _DIL_GEO[dil]
        nsub = TILE_M // qm
        for sub in range(nsub):
            ms = sub * qm
            blk = tile * nsub + sub
            var = _edge_variant(blk, m_total // qm)
            ws = pl.multiple_of(jnp.clip(blk * qm - (km - qm) // 2, 0, m_total - km), 16)
            for r0 in range(RES // nres):
                rr = tuple(range(r0, RES, RES // nres))
                q = jnp.concatenate([q_ref[r, pl.ds(ms, qm), :] for r in rr], axis=0)
                k = jnp.concatenate([k_ref[r, pl.ds(ws, km), :] for r in rr], axis=0)
                v = jnp.concatenate([v_ref[r, pl.ds(ws, km), :] for r in rr], axis=0)
                old = tuple(jnp.concatenate([ref[r, pl.ds(ms, qm), :] for r in rr], axis=0)
                            for ref in stats)
                new = _flash_step(q, k, v, t_ref[var], old)
                for ref, val in zip(stats, new):
                    for a, r in enumerate(rr):
                        ref[r, pl.ds(ms, qm), :] = val[a * qm:(a + 1) * qm]

    o_ref[...] = (acc_scr[...] / l_scr[...]).astype(BF16)


def _dil_call(qkv, t1, t4, t16):
    _, B, _, M, _ = qkv.shape
    base = 3 * H_NA
    blk_q = (None, None, RES, TILE_M, HEAD_DIM)
    blk_kv = (None, None, RES, M, HEAD_DIM)
    tspec = lambda t: pl.BlockSpec((None,) + t.shape[1:], lambda b, h, i: (h, 0, 0, 0))
    stat = pltpu.VMEM((RES, TILE_M, HEAD_DIM), F32)
    return pl.pallas_call(
        functools.partial(_dil_kernel, m_total=M),
        grid=(B, H_DIL, M // TILE_M),
        in_specs=[
            pl.BlockSpec(blk_q, lambda b, h, i: (base + h, b, 0, i, 0)),
            pl.BlockSpec(blk_kv, lambda b, h, i: (base + H_DIL + h, b, 0, 0, 0)),
            pl.BlockSpec(blk_kv, lambda b, h, i: (base + 2 * H_DIL + h, b, 0, 0, 0)),
            tspec(t1), tspec(t4), tspec(t16),
        ],
        out_specs=pl.BlockSpec(blk_q, lambda b, h, i: (h, b, 0, i, 0)),
        out_shape=jax.ShapeDtypeStruct((H_DIL, B, RES, M, HEAD_DIM), BF16),
        scratch_shapes=[stat, stat, stat],
        compiler_params=pltpu.CompilerParams(
            dimension_semantics=("parallel", "parallel", "arbitrary"), vmem_limit_bytes=VMEM_LIMIT),
        name="dil_attn",
    )(qkv, qkv, qkv, t1, t4, t16)


def _trunk(x, w_in, w_out, g_attn, g_na, g_dil, na_tables, dil_tables, g_ffn, w_gate, w_up, w_down,
           g_final, q_scale):
    B, S, D = x.shape
    M = S // RES
    R = B * S
    depth = w_in.shape[0]
    assert S % (RES * TILE_M) == 0 and M >= 512 and S % (2 * GRID_W) == 0
    rows = S // GRID_W
    x8 = jnp.transpose(x.reshape(B, M, RES, D), (0, 2, 1, 3)).reshape(R, D)
    for l in range(depth):
        qkv = _qkv_call(x8, g_attn[l][None], w_in[l], q_scale, tm=1024, tn=1024)
        qkv = qkv.reshape(qkv.shape[0], B, RES, M, HEAD_DIM)
        oa = _na_call(qkv, na_tables[(l, rows)], rows).reshape(H_NA, R, HEAD_DIM)
        ob = _dil_call(qkv, *dil_tables).reshape(H_DIL, R, HEAD_DIM)
        x1 = _outproj_call(oa, ob, g_na[l][None], g_dil[l][None], w_out[l], x8, tm=512)
        x8 = _ffn_call(x1, g_ffn[l][None], w_gate[l], w_up[l], w_down[l], g_final[None],
                       tm=512, tf=512, final_norm=(l == depth - 1))
    return jnp.transpose(x8.reshape(B, RES, M, D), (0, 2, 1, 3)).reshape(B, S, D)


def kernel(x_prompt, x_sample, w_in, w_out, g_attn, g_na, g_dil, rpb_na, t5_table, g_ffn, w_gate, w_up,
           w_down, g_final):
    depth = w_in.shape[0]
    w_na = H_NA * HEAD_DIM
    w_dil = H_DIL * HEAD_DIM
    col = np.arange(3 * (w_na + w_dil))
    is_q = (col < w_na) | ((col >= 3 * w_na) & (col < 3 * w_na + w_dil))
    q_scale = jnp.asarray(np.where(is_q, SCALE, 1.0)[None], F32)
    dil_tables = _dil_tables(t5_table)
    na_tables = {}
    for x in (x_prompt, x_sample):
        rows = x.shape[1] // GRID_W
        for l in range(depth):
            if (l, rows) not in na_tables:
                na_tables[(l, rows)] = _na_table(rpb_na[l], rows)
    wb = [w.astype(BF16) for w in (w_in, w_out, w_gate, w_up, w_down)]
    outs = []
    for x in (x_prompt, x_sample):
        outs.append(_trunk(x, wb[0], wb[1], g_attn, g_na, g_dil, na_tables, dil_tables, g_ffn,
                           wb[2], wb[3], wb[4], g_final, q_scale))
    return tuple(outs)
```

```python
import functools
import math

import jax
import jax.numpy as jnp
import numpy as np
from jax import lax
from jax.experimental import pallas as pl
from jax.experimental.pallas import tpu as pltpu

HEAD_DIM = 128
H_NA = 8
H_DIL = 8
GRID_W = 64
NA_ROWS = 8
NA_COLS = 16
DIL_PAIRS = ((128, 1), (512, 4), (2048, 16))
T5_BUCKETS = 32
T5_MAX_DIST = 2048
EPS = 1e-6
NEG = -1e30
SCALE = 1.0 / math.sqrt(HEAD_DIM)

RES = 8
RADIUS = 64
TILE_M = 256
NA_SUB_M = 16
NA_WIN_ROWS = 10
NA_WIN_M = NA_WIN_ROWS * GRID_W // RES
DIL_AHEAD = 2
VMEM_LIMIT = 56 * 1024 * 1024

BF16 = jnp.bfloat16
F32 = jnp.float32


def _dot_nt(a, b):
    return lax.dot_general(a, b, (((1,), (1,)), ((), ())), preferred_element_type=F32)


def _qkv_kernel(x_ref, g_ref, w_ref, sc_ref, o_ref, h_scr):
    @pl.when(pl.program_id(1) == 0)
    def _():
        x = x_ref[...]
        ms = jnp.mean(x * x, axis=-1, keepdims=True)
        h_scr[...] = (x * lax.rsqrt(ms + EPS) * g_ref[...]).astype(BF16)

    acc = jnp.dot(h_scr[...], w_ref[...], preferred_element_type=F32) * sc_ref[...]
    for c in range(o_ref.shape[0]):
        o_ref[c] = acc[:, c * HEAD_DIM:(c + 1) * HEAD_DIM].astype(BF16)


def _qkv_call(x, g, w, sc, tm, tn):
    R, D = x.shape
    N = w.shape[1]
    return pl.pallas_call(
        _qkv_kernel,
        grid=(R // tm, N // tn),
        in_specs=[
            pl.BlockSpec((tm, D), lambda i, j: (i, 0)),
            pl.BlockSpec((1, D), lambda i, j: (0, 0)),
            pl.BlockSpec((D, tn), lambda i, j: (0, j)),
            pl.BlockSpec((1, tn), lambda i, j: (0, j)),
        ],
        out_specs=pl.BlockSpec((tn // HEAD_DIM, tm, HEAD_DIM), lambda i, j: (j, i, 0)),
        out_shape=jax.ShapeDtypeStruct((N // HEAD_DIM, R, HEAD_DIM), BF16),
        scratch_shapes=[pltpu.VMEM((tm, D), BF16)],
        compiler_params=pltpu.CompilerParams(
            dimension_semantics=("parallel", "arbitrary"), vmem_limit_bytes=VMEM_LIMIT),
        name="qkv_proj",
    )(x, g, w, sc)


def _group_norm(ref, g):
    a = jnp.concatenate([ref[c] for c in range(ref.shape[0])], axis=1).astype(F32)
    ms = jnp.mean(a * a, axis=-1, keepdims=True)
    return (a * lax.rsqrt(ms + EPS) * g).astype(BF16)


def _outproj_kernel(oa_ref, ob_ref, ga_ref, gb_ref, w_ref, x_ref, o_ref):
    mix = jnp.concatenate([_group_norm(oa_ref, ga_ref[...]), _group_norm(ob_ref, gb_ref[...])], axis=1)
    o_ref[...] = x_ref[...] + jnp.dot(mix, w_ref[...], preferred_element_type=F32)


def _outproj_call(oa, ob, ga, gb, w, x, tm):
    R, D = x.shape
    ha, hb = oa.shape[0], ob.shape[0]
    return pl.pallas_call(
        _outproj_kernel,
        grid=(R // tm,),
        in_specs=[
            pl.BlockSpec((ha, tm, HEAD_DIM), lambda i: (0, i, 0)),
            pl.BlockSpec((hb, tm, HEAD_DIM), lambda i: (0, i, 0)),
            pl.BlockSpec((1, ha * HEAD_DIM), lambda i: (0, 0)),
            pl.BlockSpec((1, hb * HEAD_DIM), lambda i: (0, 0)),
            pl.BlockSpec(w.shape, lambda i: (0, 0)),
            pl.BlockSpec((tm, D), lambda i: (i, 0)),
        ],
        out_specs=pl.BlockSpec((tm, D), lambda i: (i, 0)),
        out_shape=jax.ShapeDtypeStruct((R, D), F32),
        compiler_params=pltpu.CompilerParams(
            dimension_semantics=("parallel",), vmem_limit_bytes=VMEM_LIMIT),
        name="out_proj",
    )(oa, ob, ga, gb, w, x)


def _ffn_kernel(x_ref, g_ref, wg_ref, wu_ref, wd_ref, gf_ref, o_ref, h_scr, *, final_norm):
    f = pl.program_id(1)

    @pl.when(f == 0)
    def _():
        x = x_ref[...]
        ms = jnp.mean(x * x, axis=-1, keepdims=True)
        h_scr[...] = (x * lax.rsqrt(ms + EPS) * g_ref[...]).astype(BF16)
        o_ref[...] = x

    h = h_scr[...]
    gate = jnp.dot(h, wg_ref[...], preferred_element_type=F32)
    up = jnp.dot(h, wu_ref[...], preferred_element_type=F32)
    act = (gate * (1.0 / (1.0 + jnp.exp(-gate))) * up).astype(BF16)
    o_ref[...] += jnp.dot(act, wd_ref[...], preferred_element_type=F32)

    if final_norm:
        @pl.when(f == pl.num_programs(1) - 1)
        def _():
            y = o_ref[...]
            ms = jnp.mean(y * y, axis=-1, keepdims=True)
            o_ref[...] = y * lax.rsqrt(ms + EPS) * gf_ref[...]


def _ffn_call(x, g, wg, wu, wd, gf, tm, tf, final_norm):
    R, D = x.shape
    F = wg.shape[1]
    return pl.pallas_call(
        functools.partial(_ffn_kernel, final_norm=final_norm),
        grid=(R // tm, F // tf),
        in_specs=[
            pl.BlockSpec((tm, D), lambda i, f: (i, 0)),
            pl.BlockSpec((1, D), lambda i, f: (0, 0)),
            pl.BlockSpec((D, tf), lambda i, f: (0, f)),
            pl.BlockSpec((D, tf), lambda i, f: (0, f)),
            pl.BlockSpec((tf, D), lambda i, f: (f, 0)),
            pl.BlockSpec((1, D), lambda i, f: (0, 0)),
        ],
        out_specs=pl.BlockSpec((tm, D), lambda i, f: (i, 0)),
        out_shape=jax.ShapeDtypeStruct((R, D), F32),
        scratch_shapes=[pltpu.VMEM((tm, D), BF16)],
        compiler_params=pltpu.CompilerParams(
            dimension_semantics=("parallel", "arbitrary"), vmem_limit_bytes=VMEM_LIMIT),
        name="ffn",
    )(x, g, wg, wu, wd, gf)


def _t5_bucket(rel):
    half = T5_BUCKETS // 2
    max_exact = half // 2
    n = np.abs(rel)
    large = max_exact + (np.log(np.maximum(n, max_exact) / max_exact)
                         / np.log(T5_MAX_DIST / max_exact) * (half - max_exact)).astype(np.int32)
    large = np.minimum(large, half - 1)
    return (rel > 0).astype(np.int32) * half + np.where(n < max_exact, n, large).astype(np.int32)


def _toeplitz(e, nq):
    length = e.shape[-1]
    f = jnp.concatenate([e, jnp.zeros(e.shape[:-1] + (1,), e.dtype)], axis=-1)
    flat = jnp.tile(f, (1,) * (e.ndim - 1) + (nq,))[..., :nq * length]
    return flat.reshape(e.shape[:-1] + (nq, length))[..., nq - 1:]


_DIL_GEO = {16: (128, 384, 1, 1), 4: (64, 128, 2, 2), 1: (32, 64, RES, RES)}
_EXT = 2048


def _dil_table(bias, dil):
    qm, km, nres, stride = _DIL_GEO[dil]
    heads = bias.shape[0]
    if dil == 16:
        band = jnp.stack([bias, jnp.full_like(bias, NEG)], axis=-1).reshape(heads, -1)[:, :4 * RADIUS + 1]
    else:
        band = bias
    half = band.shape[-1] // 2
    neg = jnp.full((heads, _EXT - half), NEG, F32)
    ext = jnp.concatenate([neg, band, neg], axis=-1)
    nq, nk = qm * stride, km * stride
    halo = (km - qm) // 2
    tabs = []
    for off in (-halo, 0, -2 * halo):
        lo = stride * off - (nq - 1) + _EXT
        t = _toeplitz(ext[:, lo:lo + nq + nk - 1], nq)
        t = t.reshape(heads, qm, nres, km, nres)
        tabs.append(jnp.transpose(t, (0, 2, 1, 4, 3)).reshape(heads, nq, nk))
    return jnp.stack(tabs, axis=1)


def _dil_tables(t5_table):
    out = []
    for _, dil in DIL_PAIRS:
        bias = t5_table[_t5_bucket(dil * np.arange(-RADIUS, RADIUS + 1))].astype(F32)
        out.append(_dil_table(bias.T, dil))
    return out


def _na_table(rpb, rows):
    npairs = rows // 2
    heads = rpb.shape[-1]
    c = np.arange(GRID_W)
    cs = np.clip(c - NA_COLS // 2, 0, GRID_W - NA_COLS)
    col_ok = (c[None, :] >= cs[:, None]) & (c[None, :] < cs[:, None] + NA_COLS)
    pad = GRID_W - NA_COLS
    e = jnp.pad(jnp.transpose(rpb, (2, 0, 1)).astype(F32), ((0, 0), (0, 0), (pad, pad)))
    cmat = jnp.where(col_ok, _toeplitz(e, GRID_W), NEG)
    dr_idx, row_ok = [], []
    for r2 in (2, 0, 1, npairs - 2, npairs - 1):
        w0 = int(np.clip(2 * r2 - 4, 0, rows - NA_WIN_ROWS))
        rq = 2 * r2 + np.arange(2)
        rk = w0 + np.arange(NA_WIN_ROWS)
        rs = np.clip(rq - NA_ROWS // 2, 0, rows - NA_ROWS)
        row_ok.append((rk[None, :] >= rs[:, None]) & (rk[None, :] < rs[:, None] + NA_ROWS))
        dr_idx.append(np.clip(rk[None, :] - rq[:, None] + NA_ROWS - 1, 0, 2 * NA_ROWS - 2))
    g = jnp.take(cmat, np.stack(dr_idx), axis=1)
    g = jnp.where(np.stack(row_ok)[..., None, None], g, NEG)
    sub = GRID_W // RES
    g = g.reshape(heads, 5, 2, NA_WIN_ROWS, sub, RES, sub, RES)
    g = jnp.transpose(g, (0, 1, 5, 2, 4, 7, 3, 6))
    return g.reshape(heads, 5, RES * NA_SUB_M, RES * NA_WIN_M)


def _na_kernel(q_ref, k_ref, v_ref, t_ref, o_ref, *, rows):
    npairs = rows // 2
    subs = TILE_M // NA_SUB_M
    tile = pl.program_id(2)

    def window(j):
        r2 = tile * subs + j
        w0 = jnp.clip(2 * r2 - 4, 0, rows - NA_WIN_ROWS)
        var = jnp.where(r2 == 0, 1, jnp.where(r2 == 1, 2, jnp.where(
            r2 == npairs - 2, 3, jnp.where(r2 == npairs - 1, 4, 0))))
        return pl.multiple_of(w0 * (GRID_W // RES), 16), var

    def scores(j):
        ws, var = window(j)
        q = jnp.concatenate([q_ref[r, pl.ds(j * NA_SUB_M, NA_SUB_M), :] for r in range(RES)], axis=0)
        k = jnp.concatenate([k_ref[r, pl.ds(ws, NA_WIN_M), :] for r in range(RES)], axis=0)
        return _dot_nt(q, k) + t_ref[var]

    s_next = scores(0)
    for j in range(subs):
        s = s_next
        if j + 1 < subs:
            s_next = scores(j + 1)
        ws, _ = window(j)
        v = jnp.concatenate([v_ref[r, pl.ds(ws, NA_WIN_M), :] for r in range(RES)], axis=0)
        m = jnp.max(s, axis=-1, keepdims=True)
        p = jnp.exp(s - m)
        l = jnp.sum(p, axis=-1, keepdims=True)
        o = jnp.dot(p.astype(BF16), v, preferred_element_type=F32) / l
        for r in range(RES):
            o_ref[r, pl.ds(j * NA_SUB_M, NA_SUB_M), :] = o[r * NA_SUB_M:(r + 1) * NA_SUB_M].astype(BF16)


def _na_call(qkv, table, rows):
    _, B, _, M, _ = qkv.shape
    blk_q = (None, None, RES, TILE_M, HEAD_DIM)
    blk_kv = (None, None, RES, M, HEAD_DIM)
    return pl.pallas_call(
        functools.partial(_na_kernel, rows=rows),
        grid=(B, H_NA, M // TILE_M),
        in_specs=[
            pl.BlockSpec(blk_q, lambda b, h, i: (h, b, 0, i, 0)),
            pl.BlockSpec(blk_kv, lambda b, h, i: (H_NA + h, b, 0, 0, 0)),
            pl.BlockSpec(blk_kv, lambda b, h, i: (2 * H_NA + h, b, 0, 0, 0)),
            pl.BlockSpec((None,) + table.shape[1:], lambda b, h, i: (h, 0, 0, 0)),
        ],
        out_specs=pl.BlockSpec(blk_q, lambda b, h, i: (h, b, 0, i, 0)),
        out_shape=jax.ShapeDtypeStruct((H_NA, B, RES, M, HEAD_DIM), BF16),
        compiler_params=pltpu.CompilerParams(
            dimension_semantics=("parallel", "parallel", "arbitrary"), vmem_limit_bytes=VMEM_LIMIT),
        name="na_attn",
    )(qkv, qkv, qkv, table)


def _edge_variant(blk, nblk):
    return jnp.where(blk == 0, 1, jnp.where(blk == nblk - 1, 2, 0))


def _flash_step(s, v, old):
    tiles = [s[:, c:c + HEAD_DIM] for c in range(0, s.shape[1], HEAD_DIM)]
    m_new = functools.reduce(jnp.maximum, tiles)
    m_new = jnp.broadcast_to(jnp.max(m_new, axis=-1, keepdims=True), m_new.shape)
    if old is not None:
        m_old, l_old, acc_old = old
        m_new = jnp.maximum(m_old, m_new)
        alpha = jnp.exp(m_old - m_new)
    p = [jnp.exp(t - m_new) for t in tiles]
    l_new = functools.reduce(jnp.add, p)
    acc = jnp.dot(jnp.concatenate(p, axis=1).astype(BF16), v, preferred_element_type=F32)
    if old is not None:
        l_new = alpha * l_old + l_new
        acc = alpha * acc_old + acc
    return m_new, l_new, acc


def _rows(ref, rr, start, size):
    return jnp.concatenate([ref[r, pl.ds(start, size), :] for r in rr], axis=0)


def _dil_kernel(q_ref, k_ref, v_ref, t1_ref, t4_ref, t16_ref, o_ref, m_scr, l_scr, acc_scr, *, m_total):
    tile = pl.program_id(2)
    stats = (m_scr, l_scr, acc_scr)

    blocks = []
    for dil, t_ref in ((16, t16_ref), (4, t4_ref), (1, t1_ref)):
        qm, km, nres, _ = _DIL_GEO[dil]
        for sub in range(TILE_M // qm):
            for r0 in range(RES // nres):
                blocks.append((t_ref, qm, km, sub, tuple(range(r0, RES, RES // nres)), dil == 16))

    def window(blk):
        _, qm, km, sub, _, _ = blk
        idx = tile * (TILE_M // qm) + sub
        ws = pl.multiple_of(jnp.clip(idx * qm - (km - qm) // 2, 0, m_total - km), 16)
        return ws, _edge_variant(idx, m_total // qm)

    def scores(blk):
        t_ref, qm, km, sub, rr, _ = blk
        ws, var = window(blk)
        return _dot_nt(_rows(q_ref, rr, sub * qm, qm), _rows(k_ref, rr, ws, km)) + t_ref[var]

    def finish(blk, s):
        _, qm, km, sub, rr, first = blk
        ws, _ = window(blk)
        old = None if first else tuple(_rows(ref, rr, sub * qm, qm) for ref in stats)
        new = _flash_step(s, _rows(v_ref, rr, ws, km), old)
        for ref, val in zip(stats, new):
            for a, r in enumerate(rr):
                ref[r, pl.ds(sub * qm, qm), :] = val[a * qm:(a + 1) * qm]

    pending = [scores(blk) for blk in blocks[:DIL_AHEAD]]
    for n, blk in enumerate(blocks):
        if n + DIL_AHEAD < len(blocks):
            pending.append(scores(blocks[n + DIL_AHEAD]))
        finish(blk, pending.pop(0))

    o_ref[...] = (acc_scr[...] / jnp.sum(l_scr[...], axis=-1, keepdims=True)).astype(BF16)


def _dil_call(qkv, t1, t4, t16):
    _, B, _, M, _ = qkv.shape
    base = 3 * H_NA
    blk_q = (None, None, RES, TILE_M, HEAD_DIM)
    blk_kv = (None, None, RES, M, HEAD_DIM)
    tspec = lambda t: pl.BlockSpec((None,) + t.shape[1:], lambda b, h, i: (h, 0, 0, 0))
    stat = pltpu.VMEM((RES, TILE_M, HEAD_DIM), F32)
    return pl.pallas_call(
        functools.partial(_dil_kernel, m_total=M),
        grid=(B, H_DIL, M // TILE_M),
        in_specs=[
            pl.BlockSpec(blk_q, lambda b, h, i: (base + h, b, 0, i, 0)),
            pl.BlockSpec(blk_kv, lambda b, h, i: (base + H_DIL + h, b, 0, 0, 0)),
            pl.BlockSpec(blk_kv, lambda b, h, i: (base + 2 * H_DIL + h, b, 0, 0, 0)),
            tspec(t1), tspec(t4), tspec(t16),
        ],
        out_specs=pl.BlockSpec(blk_q, lambda b, h, i: (h, b, 0, i, 0)),
        out_shape=jax.ShapeDtypeStruct((H_DIL, B, RES, M, HEAD_DIM), BF16),
        scratch_shapes=[stat, stat, stat],
        compiler_params=pltpu.CompilerParams(
            dimension_semantics=("parallel", "parallel", "arbitrary"), vmem_limit_bytes=VMEM_LIMIT),
        name="dil_attn",
    )(qkv, qkv, qkv, t1, t4, t16)


def _trunk(x, w_in, w_out, g_attn, g_na, g_dil, na_tables, dil_tables, g_ffn, w_gate, w_up, w_down,
           g_final, q_scale):
    B, S, D = x.shape
    M = S // RES
    R = B * S
    depth = w_in.shape[0]
    assert S % (RES * TILE_M) == 0 and M >= 512 and S % (2 * GRID_W) == 0
    rows = S // GRID_W
    x8 = jnp.transpose(x.reshape(B, M, RES, D), (0, 2, 1, 3)).reshape(R, D)
    for l in range(depth):
        qkv = _qkv_call(x8, g_attn[l][None], w_in[l], q_scale, tm=1024, tn=1024)
        qkv = qkv.reshape(qkv.shape[0], B, RES, M, HEAD_DIM)
        oa = _na_call(qkv, na_tables[(l, rows)], rows).reshape(H_NA, R, HEAD_DIM)
        ob = _dil_call(qkv, *dil_tables).reshape(H_DIL, R, HEAD_DIM)
        x1 = _outproj_call(oa, ob, g_na[l][None], g_dil[l][None], w_out[l], x8, tm=512)
        x8 = _ffn_call(x1, g_ffn[l][None], w_gate[l], w_up[l], w_down[l], g_final[None],
                       tm=512, tf=512, final_norm=(l == depth - 1))
    return jnp.transpose(x8.reshape(B, RES, M, D), (0, 2, 1, 3)).reshape(B, S, D)


def kernel(x_prompt, x_sample, w_in, w_out, g_attn, g_na, g_dil, rpb_na, t5_table, g_ffn, w_gate, w_up,
           w_down, g_final):
    depth = w_in.shape[0]
    w_na = H_NA * HEAD_DIM
    w_dil = H_DIL * HEAD_DIM
    col = np.arange(3 * (w_na + w_dil))
    is_q = (col < w_na) | ((col >= 3 * w_na) & (col < 3 * w_na + w_dil))
    q_scale = jnp.asarray(np.where(is_q, SCALE, 1.0)[None], F32)
    dil_tables = _dil_tables(t5_table)
    na_tables = {}
    for x in (x_prompt, x_sample):
        rows = x.shape[1] // GRID_W
        for l in range(depth):
            if (l, rows) not in na_tables:
                na_tables[(l, rows)] = _na_table(rpb_na[l], rows)
    wb = [w.astype(BF16) for w in (w_in, w_out, w_gate, w_up, w_down)]
    outs = []
    for x in (x_prompt, x_sample):
        outs.append(_trunk(x, wb[0], wb[1], g_attn, g_na, g_dil, na_tables, dil_tables, g_ffn,
                           wb[2], wb[3], wb[4], g_final, q_scale))
    return tuple(outs)
```

```python
import functools
import math

import jax
import jax.numpy as jnp
import numpy as np
from jax import lax
from jax.experimental import pallas as pl
from jax.experimental.pallas import tpu as pltpu

HEAD_DIM = 128
H_NA = 8
H_DIL = 8
GRID_W = 64
NA_ROWS = 8
NA_COLS = 16
DIL_PAIRS = ((128, 1), (512, 4), (2048, 16))
T5_BUCKETS = 32
T5_MAX_DIST = 2048
EPS = 1e-6
NEG = -1e30
SCALE = 1.0 / math.sqrt(HEAD_DIM)

RES = 8
RADIUS = 64
TILE_M = 256
NA_SUB_M = 16
NA_WIN_ROWS = 10
NA_WIN_M = NA_WIN_ROWS * GRID_W // RES
DIL_AHEAD = 2
VMEM_LIMIT = 56 * 1024 * 1024

BF16 = jnp.bfloat16
F32 = jnp.float32


def _dot_nt(a, b):
    return lax.dot_general(a, b, (((1,), (1,)), ((), ())), preferred_element_type=F32)


def _qkv_kernel(x_ref, g_ref, w_ref, sc_ref, o_ref, h_scr):
    @pl.when(pl.program_id(1) == 0)
    def _():
        x = x_ref[...]
        ms = jnp.mean(x * x, axis=-1, keepdims=True)
        h_scr[...] = (x * lax.rsqrt(ms + EPS) * g_ref[...]).astype(BF16)

    acc = jnp.dot(h_scr[...], w_ref[...], preferred_element_type=F32) * sc_ref[...]
    for c in range(o_ref.shape[0]):
        o_ref[c] = acc[:, c * HEAD_DIM:(c + 1) * HEAD_DIM].astype(BF16)


def _qkv_call(x, g, w, sc, tm, tn):
    R, D = x.shape
    N = w.shape[1]
    return pl.pallas_call(
        _qkv_kernel,
        grid=(R // tm, N // tn),
        in_specs=[
            pl.BlockSpec((tm, D), lambda i, j: (i, 0)),
            pl.BlockSpec((1, D), lambda i, j: (0, 0)),
            pl.BlockSpec((D, tn), lambda i, j: (0, j)),
            pl.BlockSpec((1, tn), lambda i, j: (0, j)),
        ],
        out_specs=pl.BlockSpec((tn // HEAD_DIM, tm, HEAD_DIM), lambda i, j: (j, i, 0)),
        out_shape=jax.ShapeDtypeStruct((N // HEAD_DIM, R, HEAD_DIM), BF16),
        scratch_shapes=[pltpu.VMEM((tm, D), BF16)],
        compiler_params=pltpu.CompilerParams(
            dimension_semantics=("parallel", "arbitrary"), vmem_limit_bytes=VMEM_LIMIT),
        name="qkv_proj",
    )(x, g, w, sc)


def _group_norm(ref, g):
    a = jnp.concatenate([ref[c] for c in range(ref.shape[0])], axis=1).astype(F32)
    ms = jnp.mean(a * a, axis=-1, keepdims=True)
    return (a * lax.rsqrt(ms + EPS) * g).astype(BF16)


def _mix_ffn_kernel(oa_ref, ob_ref, ga_ref, gb_ref, wo_ref, x_ref, g_ref, wg_ref, wu_ref, wd_ref, gf_ref,
                    o_ref, h_scr, *, final_norm):
    f = pl.program_id(1)

    @pl.when(f == 0)
    def _():
        mix = jnp.concatenate([_group_norm(oa_ref, ga_ref[...]), _group_norm(ob_ref, gb_ref[...])], axis=1)
        x = x_ref[...] + jnp.dot(mix, wo_ref[...], preferred_element_type=F32)
        ms = jnp.mean(x * x, axis=-1, keepdims=True)
        h_scr[...] = (x * lax.rsqrt(ms + EPS) * g_ref[...]).astype(BF16)
        o_ref[...] = x

    h = h_scr[...]
    gate = jnp.dot(h, wg_ref[...], preferred_element_type=F32)
    up = jnp.dot(h, wu_ref[...], preferred_element_type=F32)
    act = (gate * (1.0 / (1.0 + jnp.exp(-gate))) * up).astype(BF16)
    o_ref[...] += jnp.dot(act, wd_ref[...], preferred_element_type=F32)

    if final_norm:
        @pl.when(f == pl.num_programs(1) - 1)
        def _():
            y = o_ref[...]
            ms = jnp.mean(y * y, axis=-1, keepdims=True)
            o_ref[...] = y * lax.rsqrt(ms + EPS) * gf_ref[...]


def _mix_ffn_call(oa, ob, ga, gb, wo, x, g, wg, wu, wd, gf, tm, tf, final_norm):
    R, D = x.shape
    F = wg.shape[1]
    ha, hb = oa.shape[0], ob.shape[0]
    row = lambda i, f: (0, 0)
    return pl.pallas_call(
        functools.partial(_mix_ffn_kernel, final_norm=final_norm),
        grid=(R // tm, F // tf),
        in_specs=[
            pl.BlockSpec((ha, tm, HEAD_DIM), lambda i, f: (0, i, 0)),
            pl.BlockSpec((hb, tm, HEAD_DIM), lambda i, f: (0, i, 0)),
            pl.BlockSpec((1, ha * HEAD_DIM), row),
            pl.BlockSpec((1, hb * HEAD_DIM), row),
            pl.BlockSpec(wo.shape, row, pipeline_mode=pl.Buffered(1)),
            pl.BlockSpec((tm, D), lambda i, f: (i, 0)),
            pl.BlockSpec((1, D), row),
            pl.BlockSpec((D, tf), lambda i, f: (0, f)),
            pl.BlockSpec((D, tf), lambda i, f: (0, f)),
            pl.BlockSpec((tf, D), lambda i, f: (f, 0)),
            pl.BlockSpec((1, D), row),
        ],
        out_specs=pl.BlockSpec((tm, D), lambda i, f: (i, 0)),
        out_shape=jax.ShapeDtypeStruct((R, D), F32),
        scratch_shapes=[pltpu.VMEM((tm, D), BF16)],
        compiler_params=pltpu.CompilerParams(
            dimension_semantics=("parallel", "arbitrary"), vmem_limit_bytes=VMEM_LIMIT),
        name="mix_ffn",
    )(oa, ob, ga, gb, wo, x, g, wg, wu, wd, gf)


def _t5_bucket(rel):
    half = T5_BUCKETS // 2
    max_exact = half // 2
    n = np.abs(rel)
    large = max_exact + (np.log(np.maximum(n, max_exact) / max_exact)
                         / np.log(T5_MAX_DIST / max_exact) * (half - max_exact)).astype(np.int32)
    large = np.minimum(large, half - 1)
    return (rel > 0).astype(np.int32) * half + np.where(n < max_exact, n, large).astype(np.int32)


def _toeplitz(e, nq):
    length = e.shape[-1]
    f = jnp.concatenate([e, jnp.zeros(e.shape[:-1] + (1,), e.dtype)], axis=-1)
    flat = jnp.tile(f, (1,) * (e.ndim - 1) + (nq,))[..., :nq * length]
    return flat.reshape(e.shape[:-1] + (nq, length))[..., nq - 1:]


_DIL_GEO = {16: (128, 384, 1, 1), 4: (64, 128, 2, 2), 1: (32, 64, RES, RES)}
_EXT = 2048


def _permute_cols(t, perm):
    n = len(perm)
    onehot = np.zeros((n, n), np.float32)
    onehot[perm, np.arange(n)] = 1.0
    return jnp.einsum("...k,kj->...j", t, onehot, precision=lax.Precision.HIGHEST)


def _dil_table(bias, dil):
    qm, km, nres, stride = _DIL_GEO[dil]
    heads = bias.shape[0]
    if dil == 16:
        band = jnp.stack([bias, jnp.full_like(bias, NEG)], axis=-1).reshape(heads, -1)[:, :4 * RADIUS + 1]
    else:
        band = bias
    half = band.shape[-1] // 2
    neg = jnp.full((heads, _EXT - half), NEG, F32)
    ext = jnp.concatenate([neg, band, neg], axis=-1)
    nq, nk = qm * stride, km * stride
    halo = (km - qm) // 2
    tabs = []
    for off in (-halo, 0, -2 * halo):
        lo = stride * off - (nq - 1) + _EXT
        tabs.append(_toeplitz(ext[:, lo:lo + nq + nk - 1], nq))
    t = jnp.stack(tabs, axis=1)
    if nres > 1:
        t = jnp.transpose(t.reshape(heads, 3, qm, nres, nk), (0, 1, 3, 2, 4)).reshape(heads, 3, nq, nk)
        final = np.arange(nk)
        t = _permute_cols(t, (final % km) * nres + final // km)
    return t


def _dil_tables(t5_table):
    out = []
    for _, dil in DIL_PAIRS:
        bias = t5_table[_t5_bucket(dil * np.arange(-RADIUS, RADIUS + 1))].astype(F32)
        out.append(_dil_table(bias.T, dil))
    return out


def _na_table(rpb, rows):
    npairs = rows // 2
    heads = rpb.shape[-1]
    c = np.arange(GRID_W)
    cs = np.clip(c - NA_COLS // 2, 0, GRID_W - NA_COLS)
    col_ok = (c[None, :] >= cs[:, None]) & (c[None, :] < cs[:, None] + NA_COLS)
    pad = GRID_W - NA_COLS
    e = jnp.pad(jnp.transpose(rpb, (2, 0, 1)).astype(F32), ((0, 0), (0, 0), (pad, pad)))
    cmat = jnp.where(col_ok, _toeplitz(e, GRID_W), NEG)
    dr_idx, row_ok = [], []
    for r2 in (2, 0, 1, npairs - 2, npairs - 1):
        w0 = int(np.clip(2 * r2 - 4, 0, rows - NA_WIN_ROWS))
        rq = 2 * r2 + np.arange(2)
        rk = w0 + np.arange(NA_WIN_ROWS)
        rs = np.clip(rq - NA_ROWS // 2, 0, rows - NA_ROWS)
        row_ok.append((rk[None, :] >= rs[:, None]) & (rk[None, :] < rs[:, None] + NA_ROWS))
        dr_idx.append(np.clip(rk[None, :] - rq[:, None] + NA_ROWS - 1, 0, 2 * NA_ROWS - 2))
    g = jnp.take(cmat, np.stack(dr_idx), axis=1)
    g = jnp.where(np.stack(row_ok)[..., None, None], g, NEG)
    sub = GRID_W // RES
    nk = NA_WIN_ROWS * GRID_W
    g = jnp.transpose(g, (0, 1, 2, 4, 3, 5)).reshape(heads, 5, 2, sub, RES, nk)
    g = jnp.transpose(g, (0, 1, 4, 2, 3, 5)).reshape(heads, 5, RES * NA_SUB_M, nk)
    final = np.arange(nk)
    rho, rest = np.divmod(final, NA_WIN_M)
    return _permute_cols(g, (rest // sub) * GRID_W + (rest % sub) * RES + rho)


def _na_kernel(q_ref, k_ref, v_ref, t_ref, o_ref, *, rows):
    npairs = rows // 2
    subs = TILE_M // NA_SUB_M
    tile = pl.program_id(2)

    def window(j):
        r2 = tile * subs + j
        w0 = jnp.clip(2 * r2 - 4, 0, rows - NA_WIN_ROWS)
        var = jnp.where(r2 == 0, 1, jnp.where(r2 == 1, 2, jnp.where(
            r2 == npairs - 2, 3, jnp.where(r2 == npairs - 1, 4, 0))))
        return pl.multiple_of(w0 * (GRID_W // RES), 16), var

    def scores(j):
        ws, var = window(j)
        q = jnp.concatenate([q_ref[r, pl.ds(j * NA_SUB_M, NA_SUB_M), :] for r in range(RES)], axis=0)
        k = jnp.concatenate([k_ref[r, pl.ds(ws, NA_WIN_M), :] for r in range(RES)], axis=0)
        return _dot_nt(q, k) + t_ref[var]

    s_next = scores(0)
    for j in range(subs):
        s = s_next
        if j + 1 < subs:
            s_next = scores(j + 1)
        ws, _ = window(j)
        v = jnp.concatenate([v_ref[r, pl.ds(ws, NA_WIN_M), :] for r in range(RES)], axis=0)
        m = jnp.max(s, axis=-1, keepdims=True)
        p = jnp.exp(s - m)
        l = jnp.sum(p, axis=-1, keepdims=True)
        o = jnp.dot(p.astype(BF16), v, preferred_element_type=F32) / l
        for r in range(RES):
            o_ref[r, pl.ds(j * NA_SUB_M, NA_SUB_M), :] = o[r * NA_SUB_M:(r + 1) * NA_SUB_M].astype(BF16)


def _na_call(qkv, table, rows):
    _, B, _, M, _ = qkv.shape
    blk_q = (None, None, RES, TILE_M, HEAD_DIM)
    blk_kv = (None, None, RES, M, HEAD_DIM)
    return pl.pallas_call(
        functools.partial(_na_kernel, rows=rows),
        grid=(B, H_NA, M // TILE_M),
        in_specs=[
            pl.BlockSpec(blk_q, lambda b, h, i: (h, b, 0, i, 0)),
            pl.BlockSpec(blk_kv, lambda b, h, i: (H_NA + h, b, 0, 0, 0)),
            pl.BlockSpec(blk_kv, lambda b, h, i: (2 * H_NA + h, b, 0, 0, 0)),
            pl.BlockSpec((None,) + table.shape[1:], lambda b, h, i: (h, 0, 0, 0)),
        ],
        out_specs=pl.BlockSpec(blk_q, lambda b, h, i: (h, b, 0, i, 0)),
        out_shape=jax.ShapeDtypeStruct((H_NA, B, RES, M, HEAD_DIM), BF16),
        compiler_params=pltpu.CompilerParams(
            dimension_semantics=("parallel", "parallel", "arbitrary"), vmem_limit_bytes=VMEM_LIMIT),
        name="na_attn",
    )(qkv, qkv, qkv, table)


def _edge_variant(blk, nblk):
    return jnp.where(blk == 0, 1, jnp.where(blk == nblk - 1, 2, 0))


def _flash_step(s, v, old):
    tiles = [s[:, c:c + HEAD_DIM] for c in range(0, s.shape[1], HEAD_DIM)]
    m_new = functools.reduce(jnp.maximum, tiles)
    m_new = jnp.broadcast_to(jnp.max(m_new, axis=-1, keepdims=True), m_new.shape)
    if old is not None:
        m_old, l_old, acc_old = old
        m_new = jnp.maximum(m_old, m_new)
        alpha = jnp.exp(m_old - m_new)
    p = [jnp.exp(t - m_new) for t in tiles]
    l_new = functools.reduce(jnp.add, p)
    acc = jnp.dot(jnp.concatenate(p, axis=1).astype(BF16), v, preferred_element_type=F32)
    if old is not None:
        l_new = alpha * l_old + l_new
        acc = alpha * acc_old + acc
    return m_new, l_new, acc


def _rows(ref, rr, start, size):
    return jnp.concatenate([ref[r, pl.ds(start, size), :] for r in rr], axis=0)


def _dil_kernel(q_ref, k_ref, v_ref, t1_ref, t4_ref, t16_ref, o_ref, m_scr, l_scr, acc_scr, *, m_total):
    tile = pl.program_id(2)
    stats = (m_scr, l_scr, acc_scr)

    blocks = []
    for dil, t_ref in ((16, t16_ref), (4, t4_ref), (1, t1_ref)):
        qm, km, nres, _ = _DIL_GEO[dil]
        for sub in range(TILE_M // qm):
            for r0 in range(RES // nres):
                blocks.append((t_ref, qm, km, sub, tuple(range(r0, RES, RES // nres)), dil == 16))

    def window(blk):
        _, qm, km, sub, _, _ = blk
        idx = tile * (TILE_M // qm) + sub
        ws = pl.multiple_of(jnp.clip(idx * qm - (km - qm) // 2, 0, m_total - km), 16)
        return ws, _edge_variant(idx, m_total // qm)

    def scores(blk):
        t_ref, qm, km, sub, rr, _ = blk
        ws, var = window(blk)
        return _dot_nt(_rows(q_ref, rr, sub * qm, qm), _rows(k_ref, rr, ws, km)) + t_ref[var]

    def finish(blk, s):
        _, qm, km, sub, rr, first = blk
        ws, _ = window(blk)
        old = None if first else tuple(_rows(ref, rr, sub * qm, qm) for ref in stats)
        new = _flash_step(s, _rows(v_ref, rr, ws, km), old)
        for ref, val in zip(stats, new):
            for a, r in enumerate(rr):
                ref[r, pl.ds(sub * qm, qm), :] = val[a * qm:(a + 1) * qm]

    pending = [scores(blk) for blk in blocks[:DIL_AHEAD]]
    for n, blk in enumerate(blocks):
        if n + DIL_AHEAD < len(blocks):
            pending.append(scores(blocks[n + DIL_AHEAD]))
        finish(blk, pending.pop(0))

    o_ref[...] = (acc_scr[...] / jnp.sum(l_scr[...], axis=-1, keepdims=True)).astype(BF16)


def _dil_call(qkv, t1, t4, t16):
    _, B, _, M, _ = qkv.shape
    base = 3 * H_NA
    blk_q = (None, None, RES, TILE_M, HEAD_DIM)
    blk_kv = (None, None, RES, M, HEAD_DIM)
    tspec = lambda t: pl.BlockSpec((None,) + t.shape[1:], lambda b, h, i: (h, 0, 0, 0))
    stat = pltpu.VMEM((RES, TILE_M, HEAD_DIM), F32)
    return pl.pallas_call(
        functools.partial(_dil_kernel, m_total=M),
        grid=(B, H_DIL, M // TILE_M),
        in_specs=[
            pl.BlockSpec(blk_q, lambda b, h, i: (base + h, b, 0, i, 0)),
            pl.BlockSpec(blk_kv, lambda b, h, i: (base + H_DIL + h, b, 0, 0, 0)),
            pl.BlockSpec(blk_kv, lambda b, h, i: (base + 2 * H_DIL + h, b, 0, 0, 0)),
            tspec(t1), tspec(t4), tspec(t16),
        ],
        out_specs=pl.BlockSpec(blk_q, lambda b, h, i: (h, b, 0, i, 0)),
        out_shape=jax.ShapeDtypeStruct((H_DIL, B, RES, M, HEAD_DIM), BF16),
        scratch_shapes=[stat, stat, stat],
        compiler_params=pltpu.CompilerParams(
            dimension_semantics=("parallel", "parallel", "arbitrary"), vmem_limit_bytes=VMEM_LIMIT),
        name="dil_attn",
    )(qkv, qkv, qkv, t1, t4, t16)


def _trunk(x, w_in, w_out, g_attn, g_na, g_dil, na_tables, dil_tables, g_ffn, w_gate, w_up, w_down,
           g_final, q_scale):
    B, S, D = x.shape
    M = S // RES
    R = B * S
    depth = w_in.shape[0]
    assert S % (RES * TILE_M) == 0 and M >= 512 and S % (2 * GRID_W) == 0
    rows = S // GRID_W
    x8 = jnp.transpose(x.reshape(B, M, RES, D), (0, 2, 1, 3)).reshape(R, D)
    for l in range(depth):
        qkv = _qkv_call(x8, g_attn[l][None], w_in[l], q_scale, tm=1024, tn=1024)
        qkv = qkv.reshape(qkv.shape[0], B, RES, M, HEAD_DIM)
        oa = _na_call(qkv, na_tables[(l, rows)], rows).reshape(H_NA, R, HEAD_DIM)
        ob = _dil_call(qkv, *dil_tables).reshape(H_DIL, R, HEAD_DIM)
        x8 = _mix_ffn_call(oa, ob, g_na[l][None], g_dil[l][None], w_out[l], x8, g_ffn[l][None],
                           w_gate[l], w_up[l], w_down[l], g_final[None],
                           tm=512, tf=512, final_norm=(l == depth - 1))
    return jnp.transpose(x8.reshape(B, RES, M, D), (0, 2, 1, 3)).reshape(B, S, D)


def kernel(x_prompt, x_sample, w_in, w_out, g_attn, g_na, g_dil, rpb_na, t5_table, g_ffn, w_gate, w_up,
           w_down, g_final):
    depth = w_in.shape[0]
    w_na = H_NA * HEAD_DIM
    w_dil = H_DIL * HEAD_DIM
    col = np.arange(3 * (w_na + w_dil))
    is_q = (col < w_na) | ((col >= 3 * w_na) & (col < 3 * w_na + w_dil))
    q_scale = jnp.asarray(np.where(is_q, SCALE, 1.0)[None], F32)
    dil_tables = _dil_tables(t5_table)
    na_tables = {}
    for x in (x_prompt, x_sample):
        rows = x.shape[1] // GRID_W
        for l in range(depth):
            if (l, rows) not in na_tables:
                na_tables[(l, rows)] = _na_table(rpb_na[l], rows)
    wb = [w.astype(BF16) for w in (w_in, w_out, w_gate, w_up, w_down)]
    outs = []
    for x in (x_prompt, x_sample):
        outs.append(_trunk(x, wb[0], wb[1], g_attn, g_na, g_dil, na_tables, dil_tables, g_ffn,
                           wb[2], wb[3], wb[4], g_final, q_scale))
    return tuple(outs)
```

```python
import functools
import math

import jax
import jax.numpy as jnp
import numpy as np
from jax import lax
from jax.experimental import pallas as pl
from jax.experimental.pallas import tpu as pltpu

HEAD_DIM = 128
H_NA = 8
H_DIL = 8
GRID_W = 64
NA_ROWS = 8
NA_COLS = 16
DIL_PAIRS = ((128, 1), (512, 4), (2048, 16))
T5_BUCKETS = 32
T5_MAX_DIST = 2048
EPS = 1e-6
NEG = -1e30
SCALE = 1.0 / math.sqrt(HEAD_DIM)
LOG2E = math.log2(math.e)

RES = 8
RADIUS = 64
TILE_M = 256
NA_SUB_M = 16
NA_WIN_ROWS = 10
NA_WIN_M = NA_WIN_ROWS * GRID_W // RES
DIL_AHEAD = 5
NA_AHEAD = 3
VMEM_LIMIT = 56 * 1024 * 1024

BF16 = jnp.bfloat16
F32 = jnp.float32


def _dot_nt(a, b):
    return lax.dot_general(a, b, (((1,), (1,)), ((), ())), preferred_element_type=F32)


def _qkv_kernel(x_ref, g_ref, w_ref, sc_ref, o_ref, h_scr):
    @pl.when(pl.program_id(1) == 0)
    def _():
        x = x_ref[...]
        ms = jnp.mean(x * x, axis=-1, keepdims=True)
        h_scr[...] = (x * lax.rsqrt(ms + EPS) * g_ref[...]).astype(BF16)

    acc = jnp.dot(h_scr[...], w_ref[...], preferred_element_type=F32) * sc_ref[...]
    for c in range(o_ref.shape[0]):
        o_ref[c] = acc[:, c * HEAD_DIM:(c + 1) * HEAD_DIM].astype(BF16)


def _qkv_call(x, g, w, sc, tm, tn):
    R, D = x.shape
    N = w.shape[1]
    return pl.pallas_call(
        _qkv_kernel,
        grid=(R // tm, N // tn),
        in_specs=[
            pl.BlockSpec((tm, D), lambda i, j: (i, 0)),
            pl.BlockSpec((1, D), lambda i, j: (0, 0)),
            pl.BlockSpec((D, tn), lambda i, j: (0, j)),
            pl.BlockSpec((1, tn), lambda i, j: (0, j)),
        ],
        out_specs=pl.BlockSpec((tn // HEAD_DIM, tm, HEAD_DIM), lambda i, j: (j, i, 0)),
        out_shape=jax.ShapeDtypeStruct((N // HEAD_DIM, R, HEAD_DIM), BF16),
        scratch_shapes=[pltpu.VMEM((tm, D), BF16)],
        compiler_params=pltpu.CompilerParams(
            dimension_semantics=("parallel", "arbitrary"), vmem_limit_bytes=VMEM_LIMIT),
        name="qkv_proj",
    )(x, g, w, sc)


def _group_norm(ref, g):
    a = jnp.concatenate([ref[c] for c in range(ref.shape[0])], axis=1).astype(F32)
    ms = jnp.mean(a * a, axis=-1, keepdims=True)
    return (a * lax.rsqrt(ms + EPS) * g).astype(BF16)


def _mix_ffn_kernel(oa_ref, ob_ref, ga_ref, gb_ref, wo_ref, x_ref, g_ref, wg_ref, wu_ref, wd_ref, gf_ref,
                    o_ref, h_scr, *, final_norm):
    f = pl.program_id(1)

    @pl.when(f == 0)
    def _():
        mix = jnp.concatenate([_group_norm(oa_ref, ga_ref[...]), _group_norm(ob_ref, gb_ref[...])], axis=1)
        x = x_ref[...] + jnp.dot(mix, wo_ref[...], preferred_element_type=F32)
        ms = jnp.mean(x * x, axis=-1, keepdims=True)
        h_scr[...] = (x * lax.rsqrt(ms + EPS) * g_ref[...]).astype(BF16)
        o_ref[...] = x

    h = h_scr[...]
    gate = jnp.dot(h, wg_ref[...], preferred_element_type=F32)
    up = jnp.dot(h, wu_ref[...], preferred_element_type=F32)
    act = (gate * (1.0 / (1.0 + jnp.exp(-gate))) * up).astype(BF16)
    o_ref[...] += jnp.dot(act, wd_ref[...], preferred_element_type=F32)

    if final_norm:
        @pl.when(f == pl.num_programs(1) - 1)
        def _():
            y = o_ref[...]
            ms = jnp.mean(y * y, axis=-1, keepdims=True)
            o_ref[...] = y * lax.rsqrt(ms + EPS) * gf_ref[...]


def _mix_ffn_call(oa, ob, ga, gb, wo, x, g, wg, wu, wd, gf, tm, tf, final_norm):
    R, D = x.shape
    F = wg.shape[1]
    ha, hb = oa.shape[0], ob.shape[0]
    row = lambda i, f: (0, 0)
    return pl.pallas_call(
        functools.partial(_mix_ffn_kernel, final_norm=final_norm),
        grid=(R // tm, F // tf),
        in_specs=[
            pl.BlockSpec((ha, tm, HEAD_DIM), lambda i, f: (0, i, 0)),
            pl.BlockSpec((hb, tm, HEAD_DIM), lambda i, f: (0, i, 0)),
            pl.BlockSpec((1, ha * HEAD_DIM), row),
            pl.BlockSpec((1, hb * HEAD_DIM), row),
            pl.BlockSpec(wo.shape, row, pipeline_mode=pl.Buffered(1)),
            pl.BlockSpec((tm, D), lambda i, f: (i, 0)),
            pl.BlockSpec((1, D), row),
            pl.BlockSpec((D, tf), lambda i, f: (0, f)),
            pl.BlockSpec((D, tf), lambda i, f: (0, f)),
            pl.BlockSpec((tf, D), lambda i, f: (f, 0)),
            pl.BlockSpec((1, D), row),
        ],
        out_specs=pl.BlockSpec((tm, D), lambda i, f: (i, 0)),
        out_shape=jax.ShapeDtypeStruct((R, D), F32),
        scratch_shapes=[pltpu.VMEM((tm, D), BF16)],
        compiler_params=pltpu.CompilerParams(
            dimension_semantics=("parallel", "arbitrary"), vmem_limit_bytes=VMEM_LIMIT),
        name="mix_ffn",
    )(oa, ob, ga, gb, wo, x, g, wg, wu, wd, gf)


def _t5_bucket(rel):
    half = T5_BUCKETS // 2
    max_exact = half // 2
    n = np.abs(rel)
    large = max_exact + (np.log(np.maximum(n, max_exact) / max_exact)
                         / np.log(T5_MAX_DIST / max_exact) * (half - max_exact)).astype(np.int32)
    large = np.minimum(large, half - 1)
    return (rel > 0).astype(np.int32) * half + np.where(n < max_exact, n, large).astype(np.int32)


def _toeplitz(e, nq):
    length = e.shape[-1]
    f = jnp.concatenate([e, jnp.zeros(e.shape[:-1] + (1,), e.dtype)], axis=-1)
    flat = jnp.tile(f, (1,) * (e.ndim - 1) + (nq,))[..., :nq * length]
    return flat.reshape(e.shape[:-1] + (nq, length))[..., nq - 1:]


_DIL_GEO = {16: (128, 384, 1, 1), 4: (64, 128, 2, 2), 1: (32, 64, RES, RES)}
_EXT = 2048


def _permute_cols(t, perm):
    n = len(perm)
    onehot = np.zeros((n, n), np.float32)
    onehot[perm, np.arange(n)] = 1.0
    return jnp.einsum("...k,kj->...j", t, onehot, precision=lax.Precision.HIGHEST)


def _dil_table(bias, dil):
    qm, km, nres, stride = _DIL_GEO[dil]
    heads = bias.shape[0]
    if dil == 16:
        band = jnp.stack([bias, jnp.full_like(bias, NEG)], axis=-1).reshape(heads, -1)[:, :4 * RADIUS + 1]
    else:
        band = bias
    half = band.shape[-1] // 2
    neg = jnp.full((heads, _EXT - half), NEG, F32)
    ext = jnp.concatenate([neg, band, neg], axis=-1)
    nq, nk = qm * stride, km * stride
    halo = (km - qm) // 2
    tabs = []
    for off in (-halo, 0, -2 * halo):
        lo = stride * off - (nq - 1) + _EXT
        tabs.append(_toeplitz(ext[:, lo:lo + nq + nk - 1], nq))
    t = jnp.stack(tabs, axis=1)
    if nres > 1:
        t = jnp.transpose(t.reshape(heads, 3, qm, nres, nk), (0, 1, 3, 2, 4)).reshape(heads, 3, nq, nk)
        final = np.arange(nk)
        t = _permute_cols(t, (final % km) * nres + final // km)
    return t


def _dil_tables(t5_table):
    out = []
    for _, dil in DIL_PAIRS:
        bias = t5_table[_t5_bucket(dil * np.arange(-RADIUS, RADIUS + 1))].astype(F32)
        out.append(_dil_table(bias.T * LOG2E, dil))
    return out


def _na_table(rpb, rows):
    npairs = rows // 2
    heads = rpb.shape[-1]
    c = np.arange(GRID_W)
    cs = np.clip(c - NA_COLS // 2, 0, GRID_W - NA_COLS)
    col_ok = (c[None, :] >= cs[:, None]) & (c[None, :] < cs[:, None] + NA_COLS)
    pad = GRID_W - NA_COLS
    e = jnp.pad(jnp.transpose(rpb, (2, 0, 1)).astype(F32) * LOG2E, ((0, 0), (0, 0), (pad, pad)))
    cmat = jnp.where(col_ok, _toeplitz(e, GRID_W), NEG)
    dr_idx, row_ok = [], []
    for r2 in (2, 0, 1, npairs - 2, npairs - 1):
        w0 = int(np.clip(2 * r2 - 4, 0, rows - NA_WIN_ROWS))
        rq = 2 * r2 + np.arange(2)
        rk = w0 + np.arange(NA_WIN_ROWS)
        rs = np.clip(rq - NA_ROWS // 2, 0, rows - NA_ROWS)
        row_ok.append((rk[None, :] >= rs[:, None]) & (rk[None, :] < rs[:, None] + NA_ROWS))
        dr_idx.append(np.clip(rk[None, :] - rq[:, None] + NA_ROWS - 1, 0, 2 * NA_ROWS - 2))
    g = jnp.take(cmat, np.stack(dr_idx), axis=1)
    g = jnp.where(np.stack(row_ok)[..., None, None], g, NEG)
    sub = GRID_W // RES
    nk = NA_WIN_ROWS * GRID_W
    g = jnp.transpose(g, (0, 1, 2, 4, 3, 5)).reshape(heads, 5, 2, sub, RES, nk)
    g = jnp.transpose(g, (0, 1, 4, 2, 3, 5)).reshape(heads, 5, RES * NA_SUB_M, nk)
    final = np.arange(nk)
    rho, rest = np.divmod(final, NA_WIN_M)
    return _permute_cols(g, (rest // sub) * GRID_W + (rest % sub) * RES + rho)


def _na_kernel(q_ref, k_ref, v_ref, t_ref, o_ref, *, rows):
    npairs = rows // 2
    subs = TILE_M // NA_SUB_M
    tile = pl.program_id(2)

    def window(j):
        r2 = tile * subs + j
        w0 = jnp.clip(2 * r2 - 4, 0, rows - NA_WIN_ROWS)
        var = jnp.where(r2 == 0, 1, jnp.where(r2 == 1, 2, jnp.where(
            r2 == npairs - 2, 3, jnp.where(r2 == npairs - 1, 4, 0))))
        return pl.multiple_of(w0 * (GRID_W // RES), 16), var

    def scores(j):
        ws, var = window(j)
        q = jnp.concatenate([q_ref[r, pl.ds(j * NA_SUB_M, NA_SUB_M), :] for r in range(RES)], axis=0)
        k = jnp.concatenate([k_ref[r, pl.ds(ws, NA_WIN_M), :] for r in range(RES)], axis=0)
        return _dot_nt(q, k) + t_ref[var]

    pending = [scores(j) for j in range(NA_AHEAD)]
    for j in range(subs):
        if j + NA_AHEAD < subs:
            pending.append(scores(j + NA_AHEAD))
        s = pending.pop(0)
        ws, _ = window(j)
        v = jnp.concatenate([v_ref[r, pl.ds(ws, NA_WIN_M), :] for r in range(RES)], axis=0)
        m = jnp.max(s, axis=-1, keepdims=True)
        p = jnp.exp2(s - m)
        l = jnp.sum(p, axis=-1, keepdims=True)
        o = jnp.dot(p.astype(BF16), v, preferred_element_type=F32) / l
        for r in range(RES):
            o_ref[r, pl.ds(j * NA_SUB_M, NA_SUB_M), :] = o[r * NA_SUB_M:(r + 1) * NA_SUB_M].astype(BF16)


def _na_call(qkv, table, rows):
    _, B, _, M, _ = qkv.shape
    blk_q = (None, None, RES, TILE_M, HEAD_DIM)
    blk_kv = (None, None, RES, M, HEAD_DIM)
    return pl.pallas_call(
        functools.partial(_na_kernel, rows=rows),
        grid=(B, H_NA, M // TILE_M),
        in_specs=[
            pl.BlockSpec(blk_q, lambda b, h, i: (h, b, 0, i, 0)),
            pl.BlockSpec(blk_kv, lambda b, h, i: (H_NA + h, b, 0, 0, 0)),
            pl.BlockSpec(blk_kv, lambda b, h, i: (2 * H_NA + h, b, 0, 0, 0)),
            pl.BlockSpec((None,) + table.shape[1:], lambda b, h, i: (h, 0, 0, 0)),
        ],
        out_specs=pl.BlockSpec(blk_q, lambda b, h, i: (h, b, 0, i, 0)),
        out_shape=jax.ShapeDtypeStruct((H_NA, B, RES, M, HEAD_DIM), BF16),
        compiler_params=pltpu.CompilerParams(
            dimension_semantics=("parallel", "parallel", "arbitrary"), vmem_limit_bytes=VMEM_LIMIT),
        name="na_attn",
    )(qkv, qkv, qkv, table)


def _edge_variant(blk, nblk):
    return jnp.where(blk == 0, 1, jnp.where(blk == nblk - 1, 2, 0))


def _flash_step(s, v, old):
    tiles = [s[:, c:c + HEAD_DIM] for c in range(0, s.shape[1], HEAD_DIM)]
    m_new = functools.reduce(jnp.maximum, tiles)
    m_new = jnp.broadcast_to(jnp.max(m_new, axis=-1, keepdims=True), m_new.shape)
    if old is not None:
        m_old, l_old, acc_old = old
        m_new = jnp.maximum(m_old, m_new)
        alpha = jnp.exp2(m_old - m_new)
    p = [jnp.exp2(t - m_new) for t in tiles]
    l_new = functools.reduce(jnp.add, p)
    acc = jnp.dot(jnp.concatenate(p, axis=1).astype(BF16), v, preferred_element_type=F32)
    if old is not None:
        l_new = alpha * l_old + l_new
        acc = alpha * acc_old + acc
    return m_new, l_new, acc


def _rows(ref, rr, start, size):
    return jnp.concatenate([ref[r, pl.ds(start, size), :] for r in rr], axis=0)


def _dil_kernel(q_ref, k_ref, v_ref, t1_ref, t4_ref, t16_ref, o_ref, m_scr, l_scr, acc_scr, *, m_total):
    tile = pl.program_id(2)
    stats = (m_scr, l_scr, acc_scr)

    blocks = []
    for dil, t_ref in ((16, t16_ref), (4, t4_ref), (1, t1_ref)):
        qm, km, nres, _ = _DIL_GEO[dil]
        for sub in range(TILE_M // qm):
            for r0 in range(RES // nres):
                blocks.append((t_ref, qm, km, sub, tuple(range(r0, RES, RES // nres)), dil == 16))

    def window(blk):
        _, qm, km, sub, _, _ = blk
        idx = tile * (TILE_M // qm) + sub
        ws = pl.multiple_of(jnp.clip(idx * qm - (km - qm) // 2, 0, m_total - km), 16)
        return ws, _edge_variant(idx, m_total // qm)

    def scores(blk):
        t_ref, qm, km, sub, rr, _ = blk
        ws, var = window(blk)
        return _dot_nt(_rows(q_ref, rr, sub * qm, qm), _rows(k_ref, rr, ws, km)) + t_ref[var]

    def finish(blk, s):
        _, qm, km, sub, rr, first = blk
        ws, _ = window(blk)
        old = None if first else tuple(_rows(ref, rr, sub * qm, qm) for ref in stats)
        new = _flash_step(s, _rows(v_ref, rr, ws, km), old)
        for ref, val in zip(stats, new):
            for a, r in enumerate(rr):
                ref[r, pl.ds(sub * qm, qm), :] = val[a * qm:(a + 1) * qm]

    pending = [scores(blk) for blk in blocks[:DIL_AHEAD]]
    for n, blk in enumerate(blocks):
        if n + DIL_AHEAD < len(blocks):
            pending.append(scores(blocks[n + DIL_AHEAD]))
        finish(blk, pending.pop(0))

    o_ref[...] = (acc_scr[...] / jnp.sum(l_scr[...], axis=-1, keepdims=True)).astype(BF16)


def _dil_call(qkv, t1, t4, t16):
    _, B, _, M, _ = qkv.shape
    base = 3 * H_NA
    blk_q = (None, None, RES, TILE_M, HEAD_DIM)
    blk_kv = (None, None, RES, M, HEAD_DIM)
    tspec = lambda t: pl.BlockSpec((None,) + t.shape[1:], lambda b, h, i: (h, 0, 0, 0))
    stat = pltpu.VMEM((RES, TILE_M, HEAD_DIM), F32)
    return pl.pallas_call(
        functools.partial(_dil_kernel, m_total=M),
        grid=(B, H_DIL, M // TILE_M),
        in_specs=[
            pl.BlockSpec(blk_q, lambda b, h, i: (base + h, b, 0, i, 0)),
            pl.BlockSpec(blk_kv, lambda b, h, i: (base + H_DIL + h, b, 0, 0, 0)),
            pl.BlockSpec(blk_kv, lambda b, h, i: (base + 2 * H_DIL + h, b, 0, 0, 0)),
            tspec(t1), tspec(t4), tspec(t16),
        ],
        out_specs=pl.BlockSpec(blk_q, lambda b, h, i: (h, b, 0, i, 0)),
        out_shape=jax.ShapeDtypeStruct((H_DIL, B, RES, M, HEAD_DIM), BF16),
        scratch_shapes=[stat, stat, stat],
        compiler_params=pltpu.CompilerParams(
            dimension_semantics=("parallel", "parallel", "arbitrary"), vmem_limit_bytes=VMEM_LIMIT),
        name="dil_attn",
    )(qkv, qkv, qkv, t1, t4, t16)


def _trunk(x, w_in, w_out, g_attn, g_na, g_dil, na_tables, dil_tables, g_ffn, w_gate, w_up, w_down,
           g_final, q_scale):
    B, S, D = x.shape
    M = S // RES
    R = B * S
    depth = w_in.shape[0]
    assert S % (RES * TILE_M) == 0 and M >= 512 and S % (2 * GRID_W) == 0
    rows = S // GRID_W
    x8 = jnp.transpose(x.reshape(B, M, RES, D), (0, 2, 1, 3)).reshape(R, D)
    for l in range(depth):
        qkv = _qkv_call(x8, g_attn[l][None], w_in[l], q_scale, tm=1024, tn=1024)
        qkv = qkv.reshape(qkv.shape[0], B, RES, M, HEAD_DIM)
        oa = _na_call(qkv, na_tables[(l, rows)], rows).reshape(H_NA, R, HEAD_DIM)
        ob = _dil_call(qkv, *dil_tables).reshape(H_DIL, R, HEAD_DIM)
        x8 = _mix_ffn_call(oa, ob, g_na[l][None], g_dil[l][None], w_out[l], x8, g_ffn[l][None],
                           w_gate[l], w_up[l], w_down[l], g_final[None],
                           tm=512, tf=512, final_norm=(l == depth - 1))
    return jnp.transpose(x8.reshape(B, RES, M, D), (0, 2, 1, 3)).reshape(B, S, D)


def kernel(x_prompt, x_sample, w_in, w_out, g_attn, g_na, g_dil, rpb_na, t5_table, g_ffn, w_gate, w_up,
           w_down, g_final):
    depth = w_in.shape[0]
    w_na = H_NA * HEAD_DIM
    w_dil = H_DIL * HEAD_DIM
    col = np.arange(3 * (w_na + w_dil))
    is_q = (col < w_na) | ((col >= 3 * w_na) & (col < 3 * w_na + w_dil))
    q_scale = jnp.asarray(np.where(is_q, SCALE * LOG2E, 1.0)[None], F32)
    dil_tables = _dil_tables(t5_table)
    na_tables = {}
    for x in (x_prompt, x_sample):
        rows = x.shape[1] // GRID_W
        for l in range(depth):
            if (l, rows) not in na_tables:
                na_tables[(l, rows)] = _na_table(rpb_na[l], rows)
    wb = [w.astype(BF16) for w in (w_in, w_out, w_gate, w_up, w_down)]
    outs = []
    for x in (x_prompt, x_sample):
        outs.append(_trunk(x, wb[0], wb[1], g_attn, g_na, g_dil, na_tables, dil_tables, g_ffn,
                           wb[2], wb[3], wb[4], g_final, q_scale))
    return tuple(outs)
```

```python
import functools
import math

import jax
import jax.numpy as jnp
import numpy as np
from jax import lax
from jax.experimental import pallas as pl
from jax.experimental.pallas import tpu as pltpu

HEAD_DIM = 128
H_NA = 8
H_DIL = 8
GRID_W = 64
NA_ROWS = 8
NA_COLS = 16
DIL_PAIRS = ((128, 1), (512, 4), (2048, 16))
T5_BUCKETS = 32
T5_MAX_DIST = 2048
EPS = 1e-6
NEG = -1e30
SCALE = 1.0 / math.sqrt(HEAD_DIM)
LOG2E = math.log2(math.e)

RES = 8
RADIUS = 64
TILE_M = 512
NA_SUB_M = 16
NA_WIN_ROWS = 10
NA_WIN_M = NA_WIN_ROWS * GRID_W // RES
DIL_AHEAD = 5
NA_AHEAD = 3
QKV_TM, QKV_TN = 1024, 1024
FFN_TM, FFN_TF = 512, 512
VMEM_LIMIT = 56 * 1024 * 1024

BF16 = jnp.bfloat16
F32 = jnp.float32


def _dot_nt(a, b):
    return lax.dot_general(a, b, (((1,), (1,)), ((), ())), preferred_element_type=F32)


def _rmsnorm_rows(x, g):
    ms = jnp.mean(x * x, axis=-1, keepdims=True)
    return x * lax.rsqrt(ms + EPS) * g


def _qkv_kernel(x_ref, g_ref, w_ref, sc_ref, o_ref, h_scr):
    @pl.when(pl.program_id(1) == 0)
    def _():
        h_scr[...] = _rmsnorm_rows(x_ref[...], g_ref[...]).astype(BF16)

    acc = jnp.dot(h_scr[...], w_ref[...], preferred_element_type=F32) * sc_ref[...]
    for c in range(o_ref.shape[0]):
        o_ref[c] = acc[:, c * HEAD_DIM:(c + 1) * HEAD_DIM].astype(BF16)


def _qkv_call(x, g, w, sc, layer, tm, tn):
    R, D = x.shape
    N = w.shape[2]
    return pl.pallas_call(
        _qkv_kernel,
        grid=(R // tm, N // tn),
        in_specs=[
            pl.BlockSpec((tm, D), lambda i, j: (i, 0)),
            pl.BlockSpec((1, D), lambda i, j: (0, 0)),
            pl.BlockSpec((None, D, tn), lambda i, j: (layer, 0, j)),
            pl.BlockSpec((1, tn), lambda i, j: (0, j)),
        ],
        out_specs=pl.BlockSpec((tn // HEAD_DIM, tm, HEAD_DIM), lambda i, j: (j, i, 0)),
        out_shape=jax.ShapeDtypeStruct((N // HEAD_DIM, R, HEAD_DIM), BF16),
        scratch_shapes=[pltpu.VMEM((tm, D), BF16)],
        compiler_params=pltpu.CompilerParams(
            dimension_semantics=("parallel", "arbitrary"), vmem_limit_bytes=VMEM_LIMIT),
        name="qkv_proj",
    )(x, g, w, sc)


def _group_norm(ref, g):
    a = jnp.concatenate([ref[c] for c in range(ref.shape[0])], axis=1).astype(F32)
    return _rmsnorm_rows(a, g).astype(BF16)


def _mix_ffn_kernel(oa_ref, ob_ref, ga_ref, gb_ref, wo_ref, x_ref, g_ref, wg_ref, wu_ref, wd_ref, gf_ref,
                    o_ref, h_scr, *, final_norm):
    f = pl.program_id(1)

    @pl.when(f == 0)
    def _():
        mix = jnp.concatenate([_group_norm(oa_ref, ga_ref[...]), _group_norm(ob_ref, gb_ref[...])], axis=1)
        x = x_ref[...] + jnp.dot(mix, wo_ref[...], preferred_element_type=F32)
        h_scr[...] = _rmsnorm_rows(x, g_ref[...]).astype(BF16)
        o_ref[...] = x

    h = h_scr[...]
    gate = jnp.dot(h, wg_ref[...], preferred_element_type=F32)
    up = jnp.dot(h, wu_ref[...], preferred_element_type=F32)
    act = (gate * (1.0 / (1.0 + jnp.exp(-gate))) * up).astype(BF16)
    o_ref[...] += jnp.dot(act, wd_ref[...], preferred_element_type=F32)

    if final_norm:
        @pl.when(f == pl.num_programs(1) - 1)
        def _():
            o_ref[...] = _rmsnorm_rows(o_ref[...], gf_ref[...])


def _mix_ffn_call(oa, ob, ga, gb, wo, x, g, wg, wu, wd, gf, layer, tm, tf, final_norm):
    R, D = x.shape
    F = wg.shape[2]
    ha, hb = oa.shape[0], ob.shape[0]
    row = lambda i, f: (0, 0)
    return pl.pallas_call(
        functools.partial(_mix_ffn_kernel, final_norm=final_norm),
        grid=(R // tm, F // tf),
        in_specs=[
            pl.BlockSpec((ha, tm, HEAD_DIM), lambda i, f: (0, i, 0)),
            pl.BlockSpec((hb, tm, HEAD_DIM), lambda i, f: (0, i, 0)),
            pl.BlockSpec((1, ha * HEAD_DIM), row),
            pl.BlockSpec((1, hb * HEAD_DIM), row),
            pl.BlockSpec((None,) + wo.shape[1:], lambda i, f: (layer, 0, 0), pipeline_mode=pl.Buffered(1)),
            pl.BlockSpec((tm, D), lambda i, f: (i, 0)),
            pl.BlockSpec((1, D), row),
            pl.BlockSpec((None, D, tf), lambda i, f: (layer, 0, f)),
            pl.BlockSpec((None, D, tf), lambda i, f: (layer, 0, f)),
            pl.BlockSpec((None, tf, D), lambda i, f: (layer, f, 0)),
            pl.BlockSpec((1, D), row),
        ],
        out_specs=pl.BlockSpec((tm, D), lambda i, f: (i, 0)),
        out_shape=jax.ShapeDtypeStruct((R, D), F32),
        scratch_shapes=[pltpu.VMEM((tm, D), BF16)],
        compiler_params=pltpu.CompilerParams(
            dimension_semantics=("parallel", "arbitrary"), vmem_limit_bytes=VMEM_LIMIT),
        name="mix_ffn",
    )(oa, ob, ga, gb, wo, x, g, wg, wu, wd, gf)


def _t5_bucket(rel):
    half = T5_BUCKETS // 2
    max_exact = half // 2
    n = np.abs(rel)
    large = max_exact + (np.log(np.maximum(n, max_exact) / max_exact)
                         / np.log(T5_MAX_DIST / max_exact) * (half - max_exact)).astype(np.int32)
    large = np.minimum(large, half - 1)
    return (rel > 0).astype(np.int32) * half + np.where(n < max_exact, n, large).astype(np.int32)


def _toeplitz(e, nq):
    length = e.shape[-1]
    f = jnp.concatenate([e, jnp.zeros(e.shape[:-1] + (1,), e.dtype)], axis=-1)
    flat = jnp.tile(f, (1,) * (e.ndim - 1) + (nq,))[..., :nq * length]
    return flat.reshape(e.shape[:-1] + (nq, length))[..., nq - 1:]


_DIL_GEO = {16: (128, 384, 1, 1), 4: (64, 128, 2, 2), 1: (32, 64, RES, RES)}
_EXT = 2048


def _permute_cols(t, perm):
    n = len(perm)
    onehot = np.zeros((n, n), np.float32)
    onehot[perm, np.arange(n)] = 1.0
    return jnp.einsum("...k,kj->...j", t, onehot, precision=lax.Precision.HIGHEST)


def _dil_table(bias, dil):
    qm, km, nres, stride = _DIL_GEO[dil]
    heads = bias.shape[0]
    if dil == 16:
        band = jnp.stack([bias, jnp.full_like(bias, NEG)], axis=-1).reshape(heads, -1)[:, :4 * RADIUS + 1]
    else:
        band = bias
    half = band.shape[-1] // 2
    neg = jnp.full((heads, _EXT - half), NEG, F32)
    ext = jnp.concatenate([neg, band, neg], axis=-1)
    nq, nk = qm * stride, km * stride
    halo = (km - qm) // 2
    tabs = []
    for off in (-halo, 0, -2 * halo):
        lo = stride * off - (nq - 1) + _EXT
        tabs.append(_toeplitz(ext[:, lo:lo + nq + nk - 1], nq))
    t = jnp.stack(tabs, axis=1)
    if nres > 1:
        t = jnp.transpose(t.reshape(heads, 3, qm, nres, nk), (0, 1, 3, 2, 4)).reshape(heads, 3, nq, nk)
        final = np.arange(nk)
        t = _permute_cols(t, (final % km) * nres + final // km)
    return t


def _dil_tables(t5_table):
    out = []
    for _, dil in DIL_PAIRS:
        bias = t5_table[_t5_bucket(dil * np.arange(-RADIUS, RADIUS + 1))].astype(F32)
        out.append(_dil_table(bias.T * LOG2E, dil))
    return out


def _na_table(rpb, rows):
    npairs = rows // 2
    heads = rpb.shape[-1]
    c = np.arange(GRID_W)
    cs = np.clip(c - NA_COLS // 2, 0, GRID_W - NA_COLS)
    col_ok = (c[None, :] >= cs[:, None]) & (c[None, :] < cs[:, None] + NA_COLS)
    pad = GRID_W - NA_COLS
    e = jnp.pad(jnp.transpose(rpb, (2, 0, 1)).astype(F32) * LOG2E, ((0, 0), (0, 0), (pad, pad)))
    cmat = jnp.where(col_ok, _toeplitz(e, GRID_W), NEG)
    dr_idx, row_ok = [], []
    for r2 in (2, 0, 1, npairs - 2, npairs - 1):
        w0 = int(np.clip(2 * r2 - 4, 0, rows - NA_WIN_ROWS))
        rq = 2 * r2 + np.arange(2)
        rk = w0 + np.arange(NA_WIN_ROWS)
        rs = np.clip(rq - NA_ROWS // 2, 0, rows - NA_ROWS)
        row_ok.append((rk[None, :] >= rs[:, None]) & (rk[None, :] < rs[:, None] + NA_ROWS))
        dr_idx.append(np.clip(rk[None, :] - rq[:, None] + NA_ROWS - 1, 0, 2 * NA_ROWS - 2))
    g = jnp.take(cmat, np.stack(dr_idx), axis=1)
    g = jnp.where(np.stack(row_ok)[..., None, None], g, NEG)
    sub = GRID_W // RES
    nk = NA_WIN_ROWS * GRID_W
    g = jnp.transpose(g, (0, 1, 2, 4, 3, 5)).reshape(heads, 5, 2, sub, RES, nk)
    g = jnp.transpose(g, (0, 1, 4, 2, 3, 5)).reshape(heads, 5, RES * NA_SUB_M, nk)
    final = np.arange(nk)
    rho, rest = np.divmod(final, NA_WIN_M)
    return _permute_cols(g, (rest // sub) * GRID_W + (rest % sub) * RES + rho)


def _na_kernel(q_ref, k_ref, v_ref, t_ref, o_ref, *, rows):
    npairs = rows // 2
    subs = TILE_M // NA_SUB_M
    tile = pl.program_id(2)

    def window(j):
        r2 = tile * subs + j
        w0 = jnp.clip(2 * r2 - 4, 0, rows - NA_WIN_ROWS)
        var = jnp.where(r2 == 0, 1, jnp.where(r2 == 1, 2, jnp.where(
            r2 == npairs - 2, 3, jnp.where(r2 == npairs - 1, 4, 0))))
        return pl.multiple_of(w0 * (GRID_W // RES), 16), var

    def scores(j):
        ws, var = window(j)
        q = jnp.concatenate([q_ref[r, pl.ds(j * NA_SUB_M, NA_SUB_M), :] for r in range(RES)], axis=0)
        k = jnp.concatenate([k_ref[r, pl.ds(ws, NA_WIN_M), :] for r in range(RES)], axis=0)
        return _dot_nt(q, k) + t_ref[var]

    pending = [scores(j) for j in range(NA_AHEAD)]
    for j in range(subs):
        if j + NA_AHEAD < subs:
            pending.append(scores(j + NA_AHEAD))
        s = pending.pop(0)
        ws, _ = window(j)
        v = jnp.concatenate([v_ref[r, pl.ds(ws, NA_WIN_M), :] for r in range(RES)], axis=0)
        m = jnp.max(s, axis=-1, keepdims=True)
        p = jnp.exp2(s - m)
        l = jnp.sum(p, axis=-1, keepdims=True)
        o = jnp.dot(p.astype(BF16), v, preferred_element_type=F32) / l
        for r in range(RES):
            o_ref[r, pl.ds(j * NA_SUB_M, NA_SUB_M), :] = o[r * NA_SUB_M:(r + 1) * NA_SUB_M].astype(BF16)


def _na_call(qkv, table, rows):
    _, B, _, M, _ = qkv.shape
    blk_q = (None, None, RES, TILE_M, HEAD_DIM)
    blk_kv = (None, None, RES, M, HEAD_DIM)
    return pl.pallas_call(
        functools.partial(_na_kernel, rows=rows),
        grid=(B, H_NA, M // TILE_M),
        in_specs=[
            pl.BlockSpec(blk_q, lambda b, h, i: (h, b, 0, i, 0)),
            pl.BlockSpec(blk_kv, lambda b, h, i: (H_NA + h, b, 0, 0, 0)),
            pl.BlockSpec(blk_kv, lambda b, h, i: (2 * H_NA + h, b, 0, 0, 0)),
            pl.BlockSpec((None,) + table.shape[1:], lambda b, h, i: (h, 0, 0, 0)),
        ],
        out_specs=pl.BlockSpec(blk_q, lambda b, h, i: (h, b, 0, i, 0)),
        out_shape=jax.ShapeDtypeStruct((H_NA, B, RES, M, HEAD_DIM), BF16),
        compiler_params=pltpu.CompilerParams(
            dimension_semantics=("parallel", "parallel", "arbitrary"), vmem_limit_bytes=VMEM_LIMIT),
        name="na_attn",
    )(qkv, qkv, qkv, table)


def _edge_variant(blk, nblk):
    return jnp.where(blk == 0, 1, jnp.where(blk == nblk - 1, 2, 0))


def _flash_step(s, v, old):
    tiles = [s[:, c:c + HEAD_DIM] for c in range(0, s.shape[1], HEAD_DIM)]
    m_new = functools.reduce(jnp.maximum, tiles)
    m_new = jnp.broadcast_to(jnp.max(m_new, axis=-1, keepdims=True), m_new.shape)
    if old is not None:
        m_old, l_old, acc_old = old
        m_new = jnp.maximum(m_old, m_new)
        alpha = jnp.exp2(m_old - m_new)
    p = [jnp.exp2(t - m_new) for t in tiles]
    l_new = functools.reduce(jnp.add, p)
    acc = jnp.dot(jnp.concatenate(p, axis=1).astype(BF16), v, preferred_element_type=F32)
    if old is not None:
        l_new = alpha * l_old + l_new
        acc = alpha * acc_old + acc
    return m_new, l_new, acc


def _rows(ref, rr, start, size):
    return jnp.concatenate([ref[r, pl.ds(start, size), :] for r in rr], axis=0)


def _dil_kernel(q_ref, k_ref, v_ref, t1_ref, t4_ref, t16_ref, o_ref, m_scr, l_scr, acc_scr, *, m_total):
    tile = pl.program_id(2)
    stats = (m_scr, l_scr, acc_scr)

    blocks = []
    for dil, t_ref in ((16, t16_ref), (4, t4_ref), (1, t1_ref)):
        qm, km, nres, _ = _DIL_GEO[dil]
        for sub in range(TILE_M // qm):
            for r0 in range(RES // nres):
                blocks.append((t_ref, qm, km, sub, tuple(range(r0, RES, RES // nres)), dil == 16))

    def window(blk):
        _, qm, km, sub, _, _ = blk
        idx = tile * (TILE_M // qm) + sub
        ws = pl.multiple_of(jnp.clip(idx * qm - (km - qm) // 2, 0, m_total - km), 16)
        return ws, _edge_variant(idx, m_total // qm)

    def scores(blk):
        t_ref, qm, km, sub, rr, _ = blk
        ws, var = window(blk)
        return _dot_nt(_rows(q_ref, rr, sub * qm, qm), _rows(k_ref, rr, ws, km)) + t_ref[var]

    def finish(blk, s):
        _, qm, km, sub, rr, first = blk
        ws, _ = window(blk)
        old = None if first else tuple(_rows(ref, rr, sub * qm, qm) for ref in stats)
        new = _flash_step(s, _rows(v_ref, rr, ws, km), old)
        for ref, val in zip(stats, new):
            for a, r in enumerate(rr):
                ref[r, pl.ds(sub * qm, qm), :] = val[a * qm:(a + 1) * qm]

    pending = [scores(blk) for blk in blocks[:DIL_AHEAD]]
    for n, blk in enumerate(blocks):
        if n + DIL_AHEAD < len(blocks):
            pending.append(scores(blocks[n + DIL_AHEAD]))
        finish(blk, pending.pop(0))

    o_ref[...] = (acc_scr[...] / jnp.sum(l_scr[...], axis=-1, keepdims=True)).astype(BF16)


def _dil_call(qkv, t1, t4, t16):
    _, B, _, M, _ = qkv.shape
    base = 3 * H_NA
    blk_q = (None, None, RES, TILE_M, HEAD_DIM)
    blk_kv = (None, None, RES, M, HEAD_DIM)
    tspec = lambda t: pl.BlockSpec((None,) + t.shape[1:], lambda b, h, i: (h, 0, 0, 0))
    stat = pltpu.VMEM((RES, TILE_M, HEAD_DIM), F32)
    return pl.pallas_call(
        functools.partial(_dil_kernel, m_total=M),
        grid=(B, H_DIL, M // TILE_M),
        in_specs=[
            pl.BlockSpec(blk_q, lambda b, h, i: (base + h, b, 0, i, 0)),
            pl.BlockSpec(blk_kv, lambda b, h, i: (base + H_DIL + h, b, 0, 0, 0)),
            pl.BlockSpec(blk_kv, lambda b, h, i: (base + 2 * H_DIL + h, b, 0, 0, 0)),
            tspec(t1), tspec(t4), tspec(t16),
        ],
        out_specs=pl.BlockSpec(blk_q, lambda b, h, i: (h, b, 0, i, 0)),
        out_shape=jax.ShapeDtypeStruct((H_DIL, B, RES, M, HEAD_DIM), BF16),
        scratch_shapes=[stat, stat, stat],
        compiler_params=pltpu.CompilerParams(
            dimension_semantics=("parallel", "parallel", "arbitrary"), vmem_limit_bytes=VMEM_LIMIT),
        name="dil_attn",
    )(qkv, qkv, qkv, t1, t4, t16)


def _trunk(x, w_in, w_out, g_attn, g_na, g_dil, na_tables, dil_tables, g_ffn, w_gate, w_up, w_down,
           g_final, q_scale):
    B, S, D = x.shape
    M = S // RES
    R = B * S
    depth = w_in.shape[0]
    assert S % (RES * TILE_M) == 0 and M >= 512 and S % (2 * GRID_W) == 0
    rows = S // GRID_W
    x8 = jnp.transpose(x.reshape(B, M, RES, D), (0, 2, 1, 3)).reshape(R, D)
    for l in range(depth):
        qkv = _qkv_call(x8, g_attn[l][None], w_in, q_scale, layer=l, tm=QKV_TM, tn=QKV_TN)
        qkv = qkv.reshape(qkv.shape[0], B, RES, M, HEAD_DIM)
        oa = _na_call(qkv, na_tables[(l, rows)], rows).reshape(H_NA, R, HEAD_DIM)
        ob = _dil_call(qkv, *dil_tables).reshape(H_DIL, R, HEAD_DIM)
        x8 = _mix_ffn_call(oa, ob, g_na[l][None], g_dil[l][None], w_out, x8, g_ffn[l][None],
                           w_gate, w_up, w_down, g_final[None], layer=l,
                           tm=FFN_TM, tf=FFN_TF, final_norm=(l == depth - 1))
    return jnp.transpose(x8.reshape(B, RES, M, D), (0, 2, 1, 3)).reshape(B, S, D)


def kernel(x_prompt, x_sample, w_in, w_out, g_attn, g_na, g_dil, rpb_na, t5_table, g_ffn, w_gate, w_up,
           w_down, g_final):
    depth = w_in.shape[0]
    w_na = H_NA * HEAD_DIM
    w_dil = H_DIL * HEAD_DIM
    col = np.arange(3 * (w_na + w_dil))
    is_q = (col < w_na) | ((col >= 3 * w_na) & (col < 3 * w_na + w_dil))
    q_scale = jnp.asarray(np.where(is_q, SCALE * LOG2E, 1.0)[None], F32)
    dil_tables = _dil_tables(t5_table)
    na_tables = {}
    for x in (x_prompt, x_sample):
        rows = x.shape[1] // GRID_W
        for l in range(depth):
            if (l, rows) not in na_tables:
                na_tables[(l, rows)] = _na_table(rpb_na[l], rows)
    wb = [w.astype(BF16) for w in (w_in, w_out, w_gate, w_up, w_down)]
    outs = []
    for x in (x_prompt, x_sample):
        outs.append(_trunk(x, wb[0], wb[1], g_attn, g_na, g_dil, na_tables, dil_tables, g_ffn,
                           wb[2], wb[3], wb[4], g_final, q_scale))
    return tuple(outs)
```

```python
import functools
import math

import jax
import jax.numpy as jnp
import numpy as np
from jax import lax
from jax.experimental import pallas as pl
from jax.experimental.pallas import tpu as pltpu

HEAD_DIM = 128
H_NA = 8
H_DIL = 8
GRID_W = 64
NA_ROWS = 8
NA_COLS = 16
DIL_PAIRS = ((128, 1), (512, 4), (2048, 16))
T5_BUCKETS = 32
T5_MAX_DIST = 2048
EPS = 1e-6
NEG = -1e30
SCALE = 1.0 / math.sqrt(HEAD_DIM)
LOG2E = math.log2(math.e)

RES = 8
RADIUS = 64
TILE_M = 512
NA_SUB_M = 16
NA_WIN_ROWS = 10
NA_WIN_M = NA_WIN_ROWS * GRID_W // RES
DIL_AHEAD = 5
NA_AHEAD = 3
QKV_TM, QKV_TN = 1024, 1536
FFN_TM, FFN_TF = 512, 512
VMEM_LIMIT = 56 * 1024 * 1024

BF16 = jnp.bfloat16
F32 = jnp.float32


def _dot_nt(a, b):
    return lax.dot_general(a, b, (((1,), (1,)), ((), ())), preferred_element_type=F32)


def _rmsnorm_rows(x, g):
    ms = jnp.mean(x * x, axis=-1, keepdims=True)
    return x * lax.rsqrt(ms + EPS) * g


def _qkv_kernel(x_ref, g_ref, w_ref, sc_ref, o_ref, h_scr):
    @pl.when(pl.program_id(1) == 0)
    def _():
        h_scr[...] = _rmsnorm_rows(x_ref[...], g_ref[...]).astype(BF16)

    acc = jnp.dot(h_scr[...], w_ref[...], preferred_element_type=F32) * sc_ref[...]
    for c in range(o_ref.shape[0]):
        o_ref[c] = acc[:, c * HEAD_DIM:(c + 1) * HEAD_DIM].astype(BF16)


def _qkv_call(x, g, w, sc, layer, tm, tn):
    R, D = x.shape
    N = w.shape[2]
    return pl.pallas_call(
        _qkv_kernel,
        grid=(R // tm, N // tn),
        in_specs=[
            pl.BlockSpec((tm, D), lambda i, j: (i, 0)),
            pl.BlockSpec((1, D), lambda i, j: (0, 0)),
            pl.BlockSpec((None, D, tn), lambda i, j: (layer, 0, j)),
            pl.BlockSpec((1, tn), lambda i, j: (0, j)),
        ],
        out_specs=pl.BlockSpec((tn // HEAD_DIM, tm, HEAD_DIM), lambda i, j: (j, i, 0)),
        out_shape=jax.ShapeDtypeStruct((N // HEAD_DIM, R, HEAD_DIM), BF16),
        scratch_shapes=[pltpu.VMEM((tm, D), BF16)],
        compiler_params=pltpu.CompilerParams(
            dimension_semantics=("parallel", "arbitrary"), vmem_limit_bytes=VMEM_LIMIT),
        name="qkv_proj",
    )(x, g, w, sc)


def _group_norm(ref, g):
    a = jnp.concatenate([ref[c].reshape(-1, HEAD_DIM) for c in range(ref.shape[0])], axis=1).astype(F32)
    return _rmsnorm_rows(a, g).astype(BF16)


def _mix_ffn_kernel(oa_ref, ob_ref, ga_ref, gb_ref, wo_ref, x_ref, g_ref, wg_ref, wu_ref, wd_ref, gf_ref,
                    o_ref, h_scr, *nat_scr, final_norm, out_natural):
    f = pl.program_id(1)
    tm, d = h_scr.shape

    @pl.when(f == 0)
    def _():
        mix = jnp.concatenate([_group_norm(oa_ref, ga_ref[...]), _group_norm(ob_ref, gb_ref[...])], axis=1)
        x = x_ref[...].reshape(tm, d) + jnp.dot(mix, wo_ref[...], preferred_element_type=F32)
        h_scr[...] = _rmsnorm_rows(x, g_ref[...]).astype(BF16)
        o_ref[...] = x.reshape(o_ref.shape)

    h = h_scr[...]
    gate = jnp.dot(h, wg_ref[...], preferred_element_type=F32)
    up = jnp.dot(h, wu_ref[...], preferred_element_type=F32)
    act = (gate * (1.0 / (1.0 + jnp.exp(-gate))) * up).astype(BF16)
    o_ref[...] += jnp.dot(act, wd_ref[...], preferred_element_type=F32).reshape(o_ref.shape)

    if final_norm or out_natural:
        @pl.when(f == pl.num_programs(1) - 1)
        def _():
            y = o_ref[...].reshape(tm, d)
            if final_norm:
                y = _rmsnorm_rows(y, gf_ref[...])
            if not out_natural:
                o_ref[...] = y.reshape(o_ref.shape)
                return
            nat = nat_scr[0]
            sub = tm // RES
            for c in range(d // HEAD_DIM):
                for r in range(RES):
                    nat[c, pl.ds(r, sub, stride=RES), :] = y[r * sub:(r + 1) * sub,
                                                             c * HEAD_DIM:(c + 1) * HEAD_DIM]
            for c in range(d // HEAD_DIM):
                o_ref[:, c * HEAD_DIM:(c + 1) * HEAD_DIM] = nat[c]


def _mix_ffn_call(oa, ob, ga, gb, wo, x, g, wg, wu, wd, gf, layer, sub, tf, final_norm, out_natural):
    B, _, M, D = x.shape
    F = wg.shape[2]
    ha, hb = oa.shape[0], ob.shape[0]
    tm = RES * sub
    nm = M // sub
    row = lambda i, f: (0, 0)
    if out_natural:
        out_shape, out_spec = (B, RES * M, D), pl.BlockSpec((None, tm, D), lambda i, f: (i // nm, i % nm, 0))
    else:
        out_shape, out_spec = x.shape, pl.BlockSpec((None, RES, sub, D), lambda i, f: (i // nm, 0, i % nm, 0))
    scratch = [pltpu.VMEM((tm, D), BF16)]
    if out_natural:
        scratch.append(pltpu.VMEM((D // HEAD_DIM, tm, HEAD_DIM), F32))
    return pl.pallas_call(
        functools.partial(_mix_ffn_kernel, final_norm=final_norm, out_natural=out_natural),
        grid=(B * nm, F // tf),
        in_specs=[
            pl.BlockSpec((ha, None, RES, sub, HEAD_DIM), lambda i, f: (0, i // nm, 0, i % nm, 0)),
            pl.BlockSpec((hb, None, RES, sub, HEAD_DIM), lambda i, f: (0, i // nm, 0, i % nm, 0)),
            pl.BlockSpec((1, ha * HEAD_DIM), row),
            pl.BlockSpec((1, hb * HEAD_DIM), row),
            pl.BlockSpec((None,) + wo.shape[1:], lambda i, f: (layer, 0, 0), pipeline_mode=pl.Buffered(1)),
            pl.BlockSpec((None, RES, sub, D), lambda i, f: (i // nm, 0, i % nm, 0)),
            pl.BlockSpec((1, D), row),
            pl.BlockSpec((None, D, tf), lambda i, f: (layer, 0, f)),
            pl.BlockSpec((None, D, tf), lambda i, f: (layer, 0, f)),
            pl.BlockSpec((None, tf, D), lambda i, f: (layer, f, 0)),
            pl.BlockSpec((1, D), row),
        ],
        out_specs=out_spec,
        out_shape=jax.ShapeDtypeStruct(out_shape, F32),
        scratch_shapes=scratch,
        compiler_params=pltpu.CompilerParams(
            dimension_semantics=("parallel", "arbitrary"), vmem_limit_bytes=VMEM_LIMIT),
        name="mix_ffn",
    )(oa, ob, ga, gb, wo, x, g, wg, wu, wd, gf)


def _t5_bucket(rel):
    half = T5_BUCKETS // 2
    max_exact = half // 2
    n = np.abs(rel)
    large = max_exact + (np.log(np.maximum(n, max_exact) / max_exact)
                         / np.log(T5_MAX_DIST / max_exact) * (half - max_exact)).astype(np.int32)
    large = np.minimum(large, half - 1)
    return (rel > 0).astype(np.int32) * half + np.where(n < max_exact, n, large).astype(np.int32)


def _toeplitz(e, nq):
    length = e.shape[-1]
    f = jnp.concatenate([e, jnp.zeros(e.shape[:-1] + (1,), e.dtype)], axis=-1)
    flat = jnp.tile(f, (1,) * (e.ndim - 1) + (nq,))[..., :nq * length]
    return flat.reshape(e.shape[:-1] + (nq, length))[..., nq - 1:]


_DIL_GEO = {16: (128, 384, 1, 1), 4: (64, 128, 2, 2), 1: (32, 64, RES, RES)}
_EXT = 2048


def _permute_cols(t, perm):
    n = len(perm)
    onehot = np.zeros((n, n), np.float32)
    onehot[perm, np.arange(n)] = 1.0
    return jnp.einsum("...k,kj->...j", t, onehot, precision=lax.Precision.HIGHEST)


def _dil_table(bias, dil):
    qm, km, nres, stride = _DIL_GEO[dil]
    heads = bias.shape[0]
    if dil == 16:
        band = jnp.stack([bias, jnp.full_like(bias, NEG)], axis=-1).reshape(heads, -1)[:, :4 * RADIUS + 1]
    else:
        band = bias
    half = band.shape[-1] // 2
    neg = jnp.full((heads, _EXT - half), NEG, F32)
    ext = jnp.concatenate([neg, band, neg], axis=-1)
    nq, nk = qm * stride, km * stride
    halo = (km - qm) // 2
    tabs = []
    for off in (-halo, 0, -2 * halo):
        lo = stride * off - (nq - 1) + _EXT
        tabs.append(_toeplitz(ext[:, lo:lo + nq + nk - 1], nq))
    t = jnp.stack(tabs, axis=1)
    if nres > 1:
        t = jnp.transpose(t.reshape(heads, 3, qm, nres, nk), (0, 1, 3, 2, 4)).reshape(heads, 3, nq, nk)
        final = np.arange(nk)
        t = _permute_cols(t, (final % km) * nres + final // km)
    return t


def _dil_tables(t5_table):
    out = []
    for _, dil in DIL_PAIRS:
        bias = t5_table[_t5_bucket(dil * np.arange(-RADIUS, RADIUS + 1))].astype(F32)
        out.append(_dil_table(bias.T * LOG2E, dil))
    return out


def _na_table(rpb, rows):
    npairs = rows // 2
    heads = rpb.shape[-1]
    c = np.arange(GRID_W)
    cs = np.clip(c - NA_COLS // 2, 0, GRID_W - NA_COLS)
    col_ok = (c[None, :] >= cs[:, None]) & (c[None, :] < cs[:, None] + NA_COLS)
    pad = GRID_W - NA_COLS
    e = jnp.pad(jnp.transpose(rpb, (2, 0, 1)).astype(F32) * LOG2E, ((0, 0), (0, 0), (pad, pad)))
    cmat = jnp.where(col_ok, _toeplitz(e, GRID_W), NEG)
    dr_idx, row_ok = [], []
    for r2 in (2, 0, 1, npairs - 2, npairs - 1):
        w0 = int(np.clip(2 * r2 - 4, 0, rows - NA_WIN_ROWS))
        rq = 2 * r2 + np.arange(2)
        rk = w0 + np.arange(NA_WIN_ROWS)
        rs = np.clip(rq - NA_ROWS // 2, 0, rows - NA_ROWS)
        row_ok.append((rk[None, :] >= rs[:, None]) & (rk[None, :] < rs[:, None] + NA_ROWS))
        dr_idx.append(np.clip(rk[None, :] - rq[:, None] + NA_ROWS - 1, 0, 2 * NA_ROWS - 2))
    g = jnp.take(cmat, np.stack(dr_idx), axis=1)
    g = jnp.where(np.stack(row_ok)[..., None, None], g, NEG)
    sub = GRID_W // RES
    nk = NA_WIN_ROWS * GRID_W
    g = jnp.transpose(g, (0, 1, 2, 4, 3, 5)).reshape(heads, 5, 2, sub, RES, nk)
    g = jnp.transpose(g, (0, 1, 4, 2, 3, 5)).reshape(heads, 5, RES * NA_SUB_M, nk)
    final = np.arange(nk)
    rho, rest = np.divmod(final, NA_WIN_M)
    return _permute_cols(g, (rest // sub) * GRID_W + (rest % sub) * RES + rho)


def _na_kernel(q_ref, k_ref, v_ref, t_ref, o_ref, *, rows):
    npairs = rows // 2
    subs = TILE_M // NA_SUB_M
    tile = pl.program_id(2)

    def window(j):
        r2 = tile * subs + j
        w0 = jnp.clip(2 * r2 - 4, 0, rows - NA_WIN_ROWS)
        var = jnp.where(r2 == 0, 1, jnp.where(r2 == 1, 2, jnp.where(
            r2 == npairs - 2, 3, jnp.where(r2 == npairs - 1, 4, 0))))
        return pl.multiple_of(w0 * (GRID_W // RES), 16), var

    def scores(j):
        ws, var = window(j)
        q = jnp.concatenate([q_ref[r, pl.ds(j * NA_SUB_M, NA_SUB_M), :] for r in range(RES)], axis=0)
        k = jnp.concatenate([k_ref[r, pl.ds(ws, NA_WIN_M), :] for r in range(RES)], axis=0)
        return _dot_nt(q, k) + t_ref[var]

    pending = [scores(j) for j in range(NA_AHEAD)]
    for j in range(subs):
        if j + NA_AHEAD < subs:
            pending.append(scores(j + NA_AHEAD))
        s = pending.pop(0)
        ws, _ = window(j)
        v = jnp.concatenate([v_ref[r, pl.ds(ws, NA_WIN_M), :] for r in range(RES)], axis=0)
        m = jnp.max(s, axis=-1, keepdims=True)
        p = jnp.exp2(s - m)
        l = jnp.sum(p, axis=-1, keepdims=True)
        o = jnp.dot(p.astype(BF16), v, preferred_element_type=F32) / l
        for r in range(RES):
            o_ref[r, pl.ds(j * NA_SUB_M, NA_SUB_M), :] = o[r * NA_SUB_M:(r + 1) * NA_SUB_M].astype(BF16)


def _na_call(qkv, table, rows):
    _, B, _, M, _ = qkv.shape
    blk_q = (None, None, RES, TILE_M, HEAD_DIM)
    blk_kv = (None, None, RES, M, HEAD_DIM)
    return pl.pallas_call(
        functools.partial(_na_kernel, rows=rows),
        grid=(B, H_NA, M // TILE_M),
        in_specs=[
            pl.BlockSpec(blk_q, lambda b, h, i: (h, b, 0, i, 0)),
            pl.BlockSpec(blk_kv, lambda b, h, i: (H_NA + h, b, 0, 0, 0)),
            pl.BlockSpec(blk_kv, lambda b, h, i: (2 * H_NA + h, b, 0, 0, 0)),
            pl.BlockSpec((None,) + table.shape[1:], lambda b, h, i: (h, 0, 0, 0)),
        ],
        out_specs=pl.BlockSpec(blk_q, lambda b, h, i: (h, b, 0, i, 0)),
        out_shape=jax.ShapeDtypeStruct((H_NA, B, RES, M, HEAD_DIM), BF16),
        compiler_params=pltpu.CompilerParams(
            dimension_semantics=("parallel", "parallel", "arbitrary"), vmem_limit_bytes=VMEM_LIMIT),
        name="na_attn",
    )(qkv, qkv, qkv, table)


def _edge_variant(blk, nblk):
    return jnp.where(blk == 0, 1, jnp.where(blk == nblk - 1, 2, 0))


def _flash_step(s, v, old):
    tiles = [s[:, c:c + HEAD_DIM] for c in range(0, s.shape[1], HEAD_DIM)]
    m_new = functools.reduce(jnp.maximum, tiles)
    m_new = jnp.broadcast_to(jnp.max(m_new, axis=-1, keepdims=True), m_new.shape)
    if old is not None:
        m_old, l_old, acc_old = old
        m_new = jnp.maximum(m_old, m_new)
        alpha = jnp.exp2(m_old - m_new)
    p = [jnp.exp2(t - m_new) for t in tiles]
    l_new = functools.reduce(jnp.add, p)
    acc = jnp.dot(jnp.concatenate(p, axis=1).astype(BF16), v, preferred_element_type=F32)
    if old is not None:
        l_new = alpha * l_old + l_new
        acc = alpha * acc_old + acc
    return m_new, l_new, acc


def _rows(ref, rr, start, size):
    return jnp.concatenate([ref[r, pl.ds(start, size), :] for r in rr], axis=0)


def _dil_kernel(q_ref, k_ref, v_ref, t1_ref, t4_ref, t16_ref, o_ref, m_scr, l_scr, acc_scr, *, m_total):
    tile = pl.program_id(2)
    stats = (m_scr, l_scr, acc_scr)

    blocks = []
    for dil, t_ref in ((16, t16_ref), (4, t4_ref), (1, t1_ref)):
        qm, km, nres, _ = _DIL_GEO[dil]
        for sub in range(TILE_M // qm):
            for r0 in range(RES // nres):
                blocks.append((t_ref, qm, km, sub, tuple(range(r0, RES, RES // nres)), dil == 16))

    def window(blk):
        _, qm, km, sub, _, _ = blk
        idx = tile * (TILE_M // qm) + sub
        ws = pl.multiple_of(jnp.clip(idx * qm - (km - qm) // 2, 0, m_total - km), 16)
        return ws, _edge_variant(idx, m_total // qm)

    def scores(blk):
        t_ref, qm, km, sub, rr, _ = blk
        ws, var = window(blk)
        return _dot_nt(_rows(q_ref, rr, sub * qm, qm), _rows(k_ref, rr, ws, km)) + t_ref[var]

    def finish(blk, s):
        _, qm, km, sub, rr, first = blk
        ws, _ = window(blk)
        old = None if first else tuple(_rows(ref, rr, sub * qm, qm) for ref in stats)
        new = _flash_step(s, _rows(v_ref, rr, ws, km), old)
        for ref, val in zip(stats, new):
            for a, r in enumerate(rr):
                ref[r, pl.ds(sub * qm, qm), :] = val[a * qm:(a + 1) * qm]

    pending = [scores(blk) for blk in blocks[:DIL_AHEAD]]
    for n, blk in enumerate(blocks):
        if n + DIL_AHEAD < len(blocks):
            pending.append(scores(blocks[n + DIL_AHEAD]))
        finish(blk, pending.pop(0))

    o_ref[...] = (acc_scr[...] / jnp.sum(l_scr[...], axis=-1, keepdims=True)).astype(BF16)


def _dil_call(qkv, t1, t4, t16):
    _, B, _, M, _ = qkv.shape
    base = 3 * H_NA
    blk_q = (None, None, RES, TILE_M, HEAD_DIM)
    blk_kv = (None, None, RES, M, HEAD_DIM)
    tspec = lambda t: pl.BlockSpec((None,) + t.shape[1:], lambda b, h, i: (h, 0, 0, 0))
    stat = pltpu.VMEM((RES, TILE_M, HEAD_DIM), F32)
    return pl.pallas_call(
        functools.partial(_dil_kernel, m_total=M),
        grid=(B, H_DIL, M // TILE_M),
        in_specs=[
            pl.BlockSpec(blk_q, lambda b, h, i: (base + h, b, 0, i, 0)),
            pl.BlockSpec(blk_kv, lambda b, h, i: (base + H_DIL + h, b, 0, 0, 0)),
            pl.BlockSpec(blk_kv, lambda b, h, i: (base + 2 * H_DIL + h, b, 0, 0, 0)),
            tspec(t1), tspec(t4), tspec(t16),
        ],
        out_specs=pl.BlockSpec(blk_q, lambda b, h, i: (h, b, 0, i, 0)),
        out_shape=jax.ShapeDtypeStruct((H_DIL, B, RES, M, HEAD_DIM), BF16),
        scratch_shapes=[stat, stat, stat],
        compiler_params=pltpu.CompilerParams(
            dimension_semantics=("parallel", "parallel", "arbitrary"), vmem_limit_bytes=VMEM_LIMIT),
        name="dil_attn",
    )(qkv, qkv, qkv, t1, t4, t16)


def _trunk(x, w_in, w_out, g_attn, g_na, g_dil, na_tables, dil_tables, g_ffn, w_gate, w_up, w_down,
           g_final, q_scale):
    B, S, D = x.shape
    M = S // RES
    R = B * S
    depth = w_in.shape[0]
    assert S % (RES * TILE_M) == 0 and M >= 512 and S % (2 * GRID_W) == 0
    rows = S // GRID_W
    x8 = jnp.transpose(x.reshape(B, M, RES, D), (0, 2, 1, 3))
    for l in range(depth):
        last = l == depth - 1
        qkv = _qkv_call(x8.reshape(R, D), g_attn[l][None], w_in, q_scale, layer=l, tm=QKV_TM, tn=QKV_TN)
        qkv = qkv.reshape(qkv.shape[0], B, RES, M, HEAD_DIM)
        oa = _na_call(qkv, na_tables[(l, rows)], rows)
        ob = _dil_call(qkv, *dil_tables)
        x8 = _mix_ffn_call(oa, ob, g_na[l][None], g_dil[l][None], w_out, x8, g_ffn[l][None],
                           w_gate, w_up, w_down, g_final[None], layer=l,
                           sub=FFN_TM // RES, tf=FFN_TF, final_norm=last, out_natural=last)
    return x8


def kernel(x_prompt, x_sample, w_in, w_out, g_attn, g_na, g_dil, rpb_na, t5_table, g_ffn, w_gate, w_up,
           w_down, g_final):
    depth = w_in.shape[0]
    w_na = H_NA * HEAD_DIM
    w_dil = H_DIL * HEAD_DIM
    col = np.arange(3 * (w_na + w_dil))
    is_q = (col < w_na) | ((col >= 3 * w_na) & (col < 3 * w_na + w_dil))
    q_scale = jnp.asarray(np.where(is_q, SCALE * LOG2E, 1.0)[None], F32)
    dil_tables = _dil_tables(t5_table)
    na_tables = {}
    for x in (x_prompt, x_sample):
        rows = x.shape[1] // GRID_W
        for l in range(depth):
            if (l, rows) not in na_tables:
                na_tables[(l, rows)] = _na_table(rpb_na[l], rows)
    wb = [w.astype(BF16) for w in (w_in, w_out, w_gate, w_up, w_down)]
    outs = []
    for x in (x_prompt, x_sample):
        outs.append(_trunk(x, wb[0], wb[1], g_attn, g_na, g_dil, na_tables, dil_tables, g_ffn,
                           wb[2], wb[3], wb[4], g_final, q_scale))
    return tuple(outs)
```

```python
import functools
import math

import jax
import jax.numpy as jnp
import numpy as np
from jax import lax
from jax.experimental import pallas as pl
from jax.experimental.pallas import tpu as pltpu

HEAD_DIM = 128
H_NA = 8
H_DIL = 8
GRID_W = 64
NA_ROWS = 8
NA_COLS = 16
DIL_PAIRS = ((128, 1), (512, 4), (2048, 16))
T5_BUCKETS = 32
T5_MAX_DIST = 2048
EPS = 1e-6
NEG = -1e30
SCALE = 1.0 / math.sqrt(HEAD_DIM)
LOG2E = math.log2(math.e)

RES = 8
RADIUS = 64
TILE_M = 512
NA_SUB_M = 16
NA_WIN_ROWS = 10
NA_WIN_M = NA_WIN_ROWS * GRID_W // RES
DIL_AHEAD = 5
NA_AHEAD = 3
QKV_TM, QKV_TN = 1024, 2048
FFN_TM, FFN_TF = 512, 512
VMEM_LIMIT = 56 * 1024 * 1024

BF16 = jnp.bfloat16
F32 = jnp.float32


def _dot_nt(a, b):
    return lax.dot_general(a, b, (((1,), (1,)), ((), ())), preferred_element_type=F32)


def _rmsnorm_rows(x, g):
    ms = jnp.mean(x * x, axis=-1, keepdims=True)
    return x * lax.rsqrt(ms + EPS) * g


def _qkv_kernel(x_ref, g_ref, w_ref, sc_ref, o_ref, h_scr):
    @pl.when(pl.program_id(1) == 0)
    def _():
        h_scr[...] = _rmsnorm_rows(x_ref[...], g_ref[...]).astype(BF16)

    acc = jnp.dot(h_scr[...], w_ref[...], preferred_element_type=F32) * sc_ref[...]
    for c in range(o_ref.shape[0]):
        o_ref[c] = acc[:, c * HEAD_DIM:(c + 1) * HEAD_DIM].astype(BF16)


def _qkv_call(x, g, w, sc, layer, tm, tn):
    R, D = x.shape
    N = w.shape[2]
    return pl.pallas_call(
        _qkv_kernel,
        grid=(R // tm, N // tn),
        in_specs=[
            pl.BlockSpec((tm, D), lambda i, j: (i, 0)),
            pl.BlockSpec((1, D), lambda i, j: (0, 0)),
            pl.BlockSpec((None, D, tn), lambda i, j: (layer, 0, j)),
            pl.BlockSpec((1, tn), lambda i, j: (0, j)),
        ],
        out_specs=pl.BlockSpec((tn // HEAD_DIM, tm, HEAD_DIM), lambda i, j: (j, i, 0)),
        out_shape=jax.ShapeDtypeStruct((N // HEAD_DIM, R, HEAD_DIM), BF16),
        scratch_shapes=[pltpu.VMEM((tm, D), BF16)],
        compiler_params=pltpu.CompilerParams(
            dimension_semantics=("parallel", "arbitrary"), vmem_limit_bytes=VMEM_LIMIT),
        name="qkv_proj",
    )(x, g, w, sc)


def _group_norm(ref, g):
    a = jnp.concatenate([ref[c] for c in range(ref.shape[0])], axis=1).astype(F32)
    return _rmsnorm_rows(a, g).astype(BF16)


def _mix_ffn_kernel(oa_ref, ob_ref, ga_ref, gb_ref, wo_ref, x_ref, g_ref, wg_ref, wu_ref, wd_ref, gf_ref,
                    o_ref, h_scr, *, final_norm):
    f = pl.program_id(1)

    @pl.when(f == 0)
    def _():
        mix = jnp.concatenate([_group_norm(oa_ref, ga_ref[...]), _group_norm(ob_ref, gb_ref[...])], axis=1)
        x = x_ref[...] + jnp.dot(mix, wo_ref[...], preferred_element_type=F32)
        h_scr[...] = _rmsnorm_rows(x, g_ref[...]).astype(BF16)
        o_ref[...] = x

    h = h_scr[...]
    gate = jnp.dot(h, wg_ref[...], preferred_element_type=F32)
    up = jnp.dot(h, wu_ref[...], preferred_element_type=F32)
    act = (gate * (1.0 / (1.0 + jnp.exp(-gate))) * up).astype(BF16)
    o_ref[...] += jnp.dot(act, wd_ref[...], preferred_element_type=F32)

    if final_norm:
        @pl.when(f == pl.num_programs(1) - 1)
        def _():
            o_ref[...] = _rmsnorm_rows(o_ref[...], gf_ref[...])


def _mix_ffn_call(oa, ob, ga, gb, wo, x, g, wg, wu, wd, gf, layer, tm, tf, final_norm):
    R, D = x.shape
    F = wg.shape[2]
    ha, hb = oa.shape[0], ob.shape[0]
    row = lambda i, f: (0, 0)
    return pl.pallas_call(
        functools.partial(_mix_ffn_kernel, final_norm=final_norm),
        grid=(R // tm, F // tf),
        in_specs=[
            pl.BlockSpec((ha, tm, HEAD_DIM), lambda i, f: (0, i, 0)),
            pl.BlockSpec((hb, tm, HEAD_DIM), lambda i, f: (0, i, 0)),
            pl.BlockSpec((1, ha * HEAD_DIM), row),
            pl.BlockSpec((1, hb * HEAD_DIM), row),
            pl.BlockSpec((None,) + wo.shape[1:], lambda i, f: (layer, 0, 0), pipeline_mode=pl.Buffered(1)),
            pl.BlockSpec((tm, D), lambda i, f: (i, 0)),
            pl.BlockSpec((1, D), row),
            pl.BlockSpec((None, D, tf), lambda i, f: (layer, 0, f)),
            pl.BlockSpec((None, D, tf), lambda i, f: (layer, 0, f)),
            pl.BlockSpec((None, tf, D), lambda i, f: (layer, f, 0)),
            pl.BlockSpec((1, D), row),
        ],
        out_specs=pl.BlockSpec((tm, D), lambda i, f: (i, 0)),
        out_shape=jax.ShapeDtypeStruct((R, D), F32),
        scratch_shapes=[pltpu.VMEM((tm, D), BF16)],
        compiler_params=pltpu.CompilerParams(
            dimension_semantics=("parallel", "arbitrary"), vmem_limit_bytes=VMEM_LIMIT),
        name="mix_ffn",
    )(oa, ob, ga, gb, wo, x, g, wg, wu, wd, gf)


def _t5_bucket(rel):
    half = T5_BUCKETS // 2
    max_exact = half // 2
    n = np.abs(rel)
    large = max_exact + (np.log(np.maximum(n, max_exact) / max_exact)
                         / np.log(T5_MAX_DIST / max_exact) * (half - max_exact)).astype(np.int32)
    large = np.minimum(large, half - 1)
    return (rel > 0).astype(np.int32) * half + np.where(n < max_exact, n, large).astype(np.int32)


def _toeplitz(e, nq):
    length = e.shape[-1]
    f = jnp.concatenate([e, jnp.zeros(e.shape[:-1] + (1,), e.dtype)], axis=-1)
    flat = jnp.tile(f, (1,) * (e.ndim - 1) + (nq,))[..., :nq * length]
    return flat.reshape(e.shape[:-1] + (nq, length))[..., nq - 1:]


_DIL_GEO = {16: (128, 384, 1, 1), 4: (64, 128, 2, 2), 1: (32, 64, RES, RES)}
_EXT = 2048


def _permute_cols(t, perm):
    n = len(perm)
    onehot = np.zeros((n, n), np.float32)
    onehot[perm, np.arange(n)] = 1.0
    return jnp.einsum("...k,kj->...j", t, onehot, precision=lax.Precision.HIGHEST)


def _dil_table(bias, dil):
    qm, km, nres, stride = _DIL_GEO[dil]
    heads = bias.shape[0]
    if dil == 16:
        band = jnp.stack([bias, jnp.full_like(bias, NEG)], axis=-1).reshape(heads, -1)[:, :4 * RADIUS + 1]
    else:
        band = bias
    half = band.shape[-1] // 2
    neg = jnp.full((heads, _EXT - half), NEG, F32)
    ext = jnp.concatenate([neg, band, neg], axis=-1)
    nq, nk = qm * stride, km * stride
    halo = (km - qm) // 2
    tabs = []
    for off in (-halo, 0, -2 * halo):
        lo = stride * off - (nq - 1) + _EXT
        tabs.append(_toeplitz(ext[:, lo:lo + nq + nk - 1], nq))
    t = jnp.stack(tabs, axis=1)
    if nres > 1:
        t = jnp.transpose(t.reshape(heads, 3, qm, nres, nk), (0, 1, 3, 2, 4)).reshape(heads, 3, nq, nk)
        final = np.arange(nk)
        t = _permute_cols(t, (final % km) * nres + final // km)
    return t


def _dil_tables(t5_table):
    out = []
    for _, dil in DIL_PAIRS:
        bias = t5_table[_t5_bucket(dil * np.arange(-RADIUS, RADIUS + 1))].astype(F32)
        out.append(_dil_table(bias.T * LOG2E, dil))
    return out


def _na_table(rpb, rows):
    npairs = rows // 2
    heads = rpb.shape[-1]
    c = np.arange(GRID_W)
    cs = np.clip(c - NA_COLS // 2, 0, GRID_W - NA_COLS)
    col_ok = (c[None, :] >= cs[:, None]) & (c[None, :] < cs[:, None] + NA_COLS)
    pad = GRID_W - NA_COLS
    e = jnp.pad(jnp.transpose(rpb, (2, 0, 1)).astype(F32) * LOG2E, ((0, 0), (0, 0), (pad, pad)))
    cmat = jnp.where(col_ok, _toeplitz(e, GRID_W), NEG)
    dr_idx, row_ok = [], []
    for r2 in (2, 0, 1, npairs - 2, npairs - 1):
        w0 = int(np.clip(2 * r2 - 4, 0, rows - NA_WIN_ROWS))
        rq = 2 * r2 + np.arange(2)
        rk = w0 + np.arange(NA_WIN_ROWS)
        rs = np.clip(rq - NA_ROWS // 2, 0, rows - NA_ROWS)
        row_ok.append((rk[None, :] >= rs[:, None]) & (rk[None, :] < rs[:, None] + NA_ROWS))
        dr_idx.append(np.clip(rk[None, :] - rq[:, None] + NA_ROWS - 1, 0, 2 * NA_ROWS - 2))
    g = jnp.take(cmat, np.stack(dr_idx), axis=1)
    g = jnp.where(np.stack(row_ok)[..., None, None], g, NEG)
    sub = GRID_W // RES
    nk = NA_WIN_ROWS * GRID_W
    g = jnp.transpose(g, (0, 1, 2, 4, 3, 5)).reshape(heads, 5, 2, sub, RES, nk)
    g = jnp.transpose(g, (0, 1, 4, 2, 3, 5)).reshape(heads, 5, RES * NA_SUB_M, nk)
    final = np.arange(nk)
    rho, rest = np.divmod(final, NA_WIN_M)
    return _permute_cols(g, (rest // sub) * GRID_W + (rest % sub) * RES + rho)


def _na_kernel(q_ref, k_ref, v_ref, t_ref, o_ref, *, rows):
    npairs = rows // 2
    subs = TILE_M // NA_SUB_M
    tile = pl.program_id(2)

    def window(j):
        r2 = tile * subs + j
        w0 = jnp.clip(2 * r2 - 4, 0, rows - NA_WIN_ROWS)
        var = jnp.where(r2 == 0, 1, jnp.where(r2 == 1, 2, jnp.where(
            r2 == npairs - 2, 3, jnp.where(r2 == npairs - 1, 4, 0))))
        return pl.multiple_of(w0 * (GRID_W // RES), 16), var

    def scores(j):
        ws, var = window(j)
        q = jnp.concatenate([q_ref[r, pl.ds(j * NA_SUB_M, NA_SUB_M), :] for r in range(RES)], axis=0)
        k = jnp.concatenate([k_ref[r, pl.ds(ws, NA_WIN_M), :] for r in range(RES)], axis=0)
        return _dot_nt(q, k) + t_ref[var]

    pending = [scores(j) for j in range(NA_AHEAD)]
    for j in range(subs):
        if j + NA_AHEAD < subs:
            pending.append(scores(j + NA_AHEAD))
        s = pending.pop(0)
        ws, _ = window(j)
        v = jnp.concatenate([v_ref[r, pl.ds(ws, NA_WIN_M), :] for r in range(RES)], axis=0)
        m = jnp.max(s, axis=-1, keepdims=True)
        p = jnp.exp2(s - m)
        l = jnp.sum(p, axis=-1, keepdims=True)
        o = jnp.dot(p.astype(BF16), v, preferred_element_type=F32) / l
        for r in range(RES):
            o_ref[r, pl.ds(j * NA_SUB_M, NA_SUB_M), :] = o[r * NA_SUB_M:(r + 1) * NA_SUB_M].astype(BF16)


def _na_call(qkv, table, rows):
    _, B, _, M, _ = qkv.shape
    blk_q = (None, None, RES, TILE_M, HEAD_DIM)
    blk_kv = (None, None, RES, M, HEAD_DIM)
    return pl.pallas_call(
        functools.partial(_na_kernel, rows=rows),
        grid=(B, H_NA, M // TILE_M),
        in_specs=[
            pl.BlockSpec(blk_q, lambda b, h, i: (h, b, 0, i, 0)),
            pl.BlockSpec(blk_kv, lambda b, h, i: (H_NA + h, b, 0, 0, 0)),
            pl.BlockSpec(blk_kv, lambda b, h, i: (2 * H_NA + h, b, 0, 0, 0)),
            pl.BlockSpec((None,) + table.shape[1:], lambda b, h, i: (h, 0, 0, 0)),
        ],
        out_specs=pl.BlockSpec(blk_q, lambda b, h, i: (h, b, 0, i, 0)),
        out_shape=jax.ShapeDtypeStruct((H_NA, B, RES, M, HEAD_DIM), BF16),
        compiler_params=pltpu.CompilerParams(
            dimension_semantics=("parallel", "parallel", "arbitrary"), vmem_limit_bytes=VMEM_LIMIT),
        name="na_attn",
    )(qkv, qkv, qkv, table)


def _edge_variant(blk, nblk):
    return jnp.where(blk == 0, 1, jnp.where(blk == nblk - 1, 2, 0))


def _flash_step(s, v, old):
    tiles = [s[:, c:c + HEAD_DIM] for c in range(0, s.shape[1], HEAD_DIM)]
    m_new = functools.reduce(jnp.maximum, tiles)
    m_new = jnp.broadcast_to(jnp.max(m_new, axis=-1, keepdims=True), m_new.shape)
    if old is not None:
        m_old, l_old, acc_old = old
        m_new = jnp.maximum(m_old, m_new)
        alpha = jnp.exp2(m_old - m_new)
    p = [jnp.exp2(t - m_new) for t in tiles]
    l_new = functools.reduce(jnp.add, p)
    acc = jnp.dot(jnp.concatenate(p, axis=1).astype(BF16), v, preferred_element_type=F32)
    if old is not None:
        l_new = alpha * l_old + l_new
        acc = alpha * acc_old + acc
    return m_new, l_new, acc


def _rows(ref, rr, start, size):
    return jnp.concatenate([ref[r, pl.ds(start, size), :] for r in rr], axis=0)


def _dil_kernel(q_ref, k_ref, v_ref, t1_ref, t4_ref, t16_ref, o_ref, m_scr, l_scr, acc_scr, *, m_total):
    tile = pl.program_id(2)
    stats = (m_scr, l_scr, acc_scr)

    blocks = []
    for dil, t_ref in ((16, t16_ref), (4, t4_ref), (1, t1_ref)):
        qm, km, nres, _ = _DIL_GEO[dil]
        for sub in range(TILE_M // qm):
            for r0 in range(RES // nres):
                blocks.append((t_ref, qm, km, sub, tuple(range(r0, RES, RES // nres)), dil == 16))

    def window(blk):
        _, qm, km, sub, _, _ = blk
        idx = tile * (TILE_M // qm) + sub
        ws = pl.multiple_of(jnp.clip(idx * qm - (km - qm) // 2, 0, m_total - km), 16)
        return ws, _edge_variant(idx, m_total // qm)

    def scores(blk):
        t_ref, qm, km, sub, rr, _ = blk
        ws, var = window(blk)
        return _dot_nt(_rows(q_ref, rr, sub * qm, qm), _rows(k_ref, rr, ws, km)) + t_ref[var]

    def finish(blk, s):
        _, qm, km, sub, rr, first = blk
        ws, _ = window(blk)
        old = None if first else tuple(_rows(ref, rr, sub * qm, qm) for ref in stats)
        new = _flash_step(s, _rows(v_ref, rr, ws, km), old)
        for ref, val in zip(stats, new):
            for a, r in enumerate(rr):
                ref[r, pl.ds(sub * qm, qm), :] = val[a * qm:(a + 1) * qm]

    pending = [scores(blk) for blk in blocks[:DIL_AHEAD]]
    for n, blk in enumerate(blocks):
        if n + DIL_AHEAD < len(blocks):
            pending.append(scores(blocks[n + DIL_AHEAD]))
        finish(blk, pending.pop(0))

    o_ref[...] = (acc_scr[...] / jnp.sum(l_scr[...], axis=-1, keepdims=True)).astype(BF16)


def _dil_call(qkv, t1, t4, t16):
    _, B, _, M, _ = qkv.shape
    base = 3 * H_NA
    blk_q = (None, None, RES, TILE_M, HEAD_DIM)
    blk_kv = (None, None, RES, M, HEAD_DIM)
    tspec = lambda t: pl.BlockSpec((None,) + t.shape[1:], lambda b, h, i: (h, 0, 0, 0))
    stat = pltpu.VMEM((RES, TILE_M, HEAD_DIM), F32)
    return pl.pallas_call(
        functools.partial(_dil_kernel, m_total=M),
        grid=(B, H_DIL, M // TILE_M),
        in_specs=[
            pl.BlockSpec(blk_q, lambda b, h, i: (base + h, b, 0, i, 0)),
            pl.BlockSpec(blk_kv, lambda b, h, i: (base + H_DIL + h, b, 0, 0, 0)),
            pl.BlockSpec(blk_kv, lambda b, h, i: (base + 2 * H_DIL + h, b, 0, 0, 0)),
            tspec(t1), tspec(t4), tspec(t16),
        ],
        out_specs=pl.BlockSpec(blk_q, lambda b, h, i: (h, b, 0, i, 0)),
        out_shape=jax.ShapeDtypeStruct((H_DIL, B, RES, M, HEAD_DIM), BF16),
        scratch_shapes=[stat, stat, stat],
        compiler_params=pltpu.CompilerParams(
            dimension_semantics=("parallel", "parallel", "arbitrary"), vmem_limit_bytes=VMEM_LIMIT),
        name="dil_attn",
    )(qkv, qkv, qkv, t1, t4, t16)


def _trunk(x, w_in, w_out, g_attn, g_na, g_dil, na_tables, dil_tables, g_ffn, w_gate, w_up, w_down,
           g_final, q_scale):
    B, S, D = x.shape
    M = S // RES
    R = B * S
    depth = w_in.shape[0]
    assert S % (RES * TILE_M) == 0 and M >= 512 and S % (2 * GRID_W) == 0
    rows = S // GRID_W
    x8 = jnp.transpose(x.reshape(B, M, RES, D), (0, 2, 1, 3)).reshape(R, D)
    for l in range(depth):
        qkv = _qkv_call(x8, g_attn[l][None], w_in, q_scale, layer=l, tm=QKV_TM, tn=QKV_TN)
        qkv = qkv.reshape(qkv.shape[0], B, RES, M, HEAD_DIM)
        oa = _na_call(qkv, na_tables[(l, rows)], rows).reshape(H_NA, R, HEAD_DIM)
        ob = _dil_call(qkv, *dil_tables).reshape(H_DIL, R, HEAD_DIM)
        x8 = _mix_ffn_call(oa, ob, g_na[l][None], g_dil[l][None], w_out, x8, g_ffn[l][None],
                           w_gate, w_up, w_down, g_final[None], layer=l,
                           tm=FFN_TM, tf=FFN_TF, final_norm=(l == depth - 1))
    return jnp.transpose(x8.reshape(B, RES, M, D), (0, 2, 1, 3)).reshape(B, S, D)


def kernel(x_prompt, x_sample, w_in, w_out, g_attn, g_na, g_dil, rpb_na, t5_table, g_ffn, w_gate, w_up,
           w_down, g_final):
    depth = w_in.shape[0]
    w_na = H_NA * HEAD_DIM
    w_dil = H_DIL * HEAD_DIM
    col = np.arange(3 * (w_na + w_dil))
    is_q = (col < w_na) | ((col >= 3 * w_na) & (col < 3 * w_na + w_dil))
    q_scale = jnp.asarray(np.where(is_q, SCALE * LOG2E, 1.0)[None], F32)
    dil_tables = _dil_tables(t5_table)
    na_tables = {}
    for x in (x_prompt, x_sample):
        rows = x.shape[1] // GRID_W
        for l in range(depth):
            if (l, rows) not in na_tables:
                na_tables[(l, rows)] = _na_table(rpb_na[l], rows)
    wb = [w.astype(BF16) for w in (w_in, w_out, w_gate, w_up, w_down)]
    outs = []
    for x in (x_prompt, x_sample):
        outs.append(_trunk(x, wb[0], wb[1], g_attn, g_na, g_dil, na_tables, dil_tables, g_ffn,
                           wb[2], wb[3], wb[4], g_final, q_scale))
    return tuple(outs)
```

```python
import functools
import math

import jax
import jax.numpy as jnp
import numpy as np
from jax import lax
from jax.experimental import pallas as pl
from jax.experimental.pallas import tpu as pltpu

HEAD_DIM = 128
H_NA = 8
H_DIL = 8
GRID_W = 64
NA_ROWS = 8
NA_COLS = 16
DIL_PAIRS = ((128, 1), (512, 4), (2048, 16))
T5_BUCKETS = 32
T5_MAX_DIST = 2048
EPS = 1e-6
NEG = -1e30
SCALE = 1.0 / math.sqrt(HEAD_DIM)
LOG2E = math.log2(math.e)

RES = 8
RADIUS = 64
TILE_M = 512
NA_SUB_M = 16
NA_WIN_ROWS = 10
NA_WIN_M = NA_WIN_ROWS * GRID_W // RES
DIL_AHEAD = 5
NA_AHEAD = 3
QKV_TM, QKV_TN = 1024, 2048
OUTPROJ_TM = 512
FFN_UP_TM, FFN_UP_TF = 2048, 512
FFN_UP_CHUNKS = 8
FFN_DOWN_TM = 512
VMEM_LIMIT = 56 * 1024 * 1024

BF16 = jnp.bfloat16
F32 = jnp.float32


def _dot_nt(a, b):
    return lax.dot_general(a, b, (((1,), (1,)), ((), ())), preferred_element_type=F32)


def _rmsnorm_rows(x, g):
    ms = jnp.mean(x * x, axis=-1, keepdims=True)
    return x * lax.rsqrt(ms + EPS) * g


def _qkv_kernel(x_ref, g_ref, w_ref, sc_ref, o_ref, h_scr):
    @pl.when(pl.program_id(1) == 0)
    def _():
        h_scr[...] = _rmsnorm_rows(x_ref[...], g_ref[...]).astype(BF16)

    acc = jnp.dot(h_scr[...], w_ref[...], preferred_element_type=F32) * sc_ref[...]
    for c in range(o_ref.shape[0]):
        o_ref[c] = acc[:, c * HEAD_DIM:(c + 1) * HEAD_DIM].astype(BF16)


def _qkv_call(x, g, w, sc, layer, tm, tn):
    R, D = x.shape
    N = w.shape[2]
    return pl.pallas_call(
        _qkv_kernel,
        grid=(R // tm, N // tn),
        in_specs=[
            pl.BlockSpec((tm, D), lambda i, j: (i, 0)),
            pl.BlockSpec((1, D), lambda i, j: (0, 0)),
            pl.BlockSpec((None, D, tn), lambda i, j: (layer, 0, j)),
            pl.BlockSpec((1, tn), lambda i, j: (0, j)),
        ],
        out_specs=pl.BlockSpec((tn // HEAD_DIM, tm, HEAD_DIM), lambda i, j: (j, i, 0)),
        out_shape=jax.ShapeDtypeStruct((N // HEAD_DIM, R, HEAD_DIM), BF16),
        scratch_shapes=[pltpu.VMEM((tm, D), BF16)],
        compiler_params=pltpu.CompilerParams(
            dimension_semantics=("parallel", "arbitrary"), vmem_limit_bytes=VMEM_LIMIT),
        name="qkv_proj",
    )(x, g, w, sc)


def _group_norm(ref, g):
    a = jnp.concatenate([ref[c] for c in range(ref.shape[0])], axis=1).astype(F32)
    return _rmsnorm_rows(a, g).astype(BF16)


def _outproj_kernel(oa_ref, ob_ref, ga_ref, gb_ref, wo_ref, x_ref, g_ref, x1_ref, h_ref):
    mix = jnp.concatenate([_group_norm(oa_ref, ga_ref[...]), _group_norm(ob_ref, gb_ref[...])], axis=1)
    x = x_ref[...] + jnp.dot(mix, wo_ref[...], preferred_element_type=F32)
    x1_ref[...] = x
    h_ref[...] = _rmsnorm_rows(x, g_ref[...]).astype(BF16)


def _outproj_call(oa, ob, ga, gb, wo, x, g, layer, tm):
    R, D = x.shape
    ha, hb = oa.shape[0], ob.shape[0]
    row = lambda i: (0, 0)
    return pl.pallas_call(
        _outproj_kernel,
        grid=(R // tm,),
        in_specs=[
            pl.BlockSpec((ha, tm, HEAD_DIM), lambda i: (0, i, 0)),
            pl.BlockSpec((hb, tm, HEAD_DIM), lambda i: (0, i, 0)),
            pl.BlockSpec((1, ha * HEAD_DIM), row),
            pl.BlockSpec((1, hb * HEAD_DIM), row),
            pl.BlockSpec((None,) + wo.shape[1:], lambda i: (layer, 0, 0), pipeline_mode=pl.Buffered(1)),
            pl.BlockSpec((tm, D), lambda i: (i, 0)),
            pl.BlockSpec((1, D), row),
        ],
        out_specs=[pl.BlockSpec((tm, D), lambda i: (i, 0)), pl.BlockSpec((tm, D), lambda i: (i, 0))],
        out_shape=[jax.ShapeDtypeStruct((R, D), F32), jax.ShapeDtypeStruct((R, D), BF16)],
        compiler_params=pltpu.CompilerParams(
            dimension_semantics=("parallel",), vmem_limit_bytes=VMEM_LIMIT),
        name="out_proj",
    )(oa, ob, ga, gb, wo, x, g)


def _ffn_up_kernel(h_ref, wg_ref, wu_ref, a_ref):
    rows = h_ref.shape[0] // FFN_UP_CHUNKS
    for k in range(FFN_UP_CHUNKS):
        h = h_ref[k * rows:(k + 1) * rows]
        gate = jnp.dot(h, wg_ref[...], preferred_element_type=F32)
        up = jnp.dot(h, wu_ref[...], preferred_element_type=F32)
        a_ref[k * rows:(k + 1) * rows] = (gate * (1.0 / (1.0 + jnp.exp(-gate))) * up).astype(BF16)


def _ffn_up_call(h, wg, wu, layer, tm, tf):
    R, D = h.shape
    F = wg.shape[2]
    return pl.pallas_call(
        _ffn_up_kernel,
        grid=(R // tm, F // tf),
        in_specs=[
            pl.BlockSpec((tm, D), lambda i, f: (i, 0)),
            pl.BlockSpec((None, D, tf), lambda i, f: (layer, 0, f)),
            pl.BlockSpec((None, D, tf), lambda i, f: (layer, 0, f)),
        ],
        out_specs=pl.BlockSpec((tm, tf), lambda i, f: (i, f)),
        out_shape=jax.ShapeDtypeStruct((R, F), BF16),
        compiler_params=pltpu.CompilerParams(
            dimension_semantics=("parallel", "arbitrary"), vmem_limit_bytes=VMEM_LIMIT),
        name="ffn_up",
    )(h, wg, wu)


def _ffn_down_kernel(a_ref, wd_ref, x_ref, gf_ref, o_ref, *, final_norm):
    y = x_ref[...] + jnp.dot(a_ref[...], wd_ref[...], preferred_element_type=F32)
    if final_norm:
        y = _rmsnorm_rows(y, gf_ref[...])
    o_ref[...] = y


def _ffn_down_call(act, wd, x1, gf, layer, tm, final_norm):
    R, F = act.shape
    D = x1.shape[1]
    return pl.pallas_call(
        functools.partial(_ffn_down_kernel, final_norm=final_norm),
        grid=(R // tm,),
        in_specs=[
            pl.BlockSpec((tm, F), lambda i: (i, 0)),
            pl.BlockSpec((None, F, D), lambda i: (layer, 0, 0), pipeline_mode=pl.Buffered(1)),
            pl.BlockSpec((tm, D), lambda i: (i, 0)),
            pl.BlockSpec((1, D), lambda i: (0, 0)),
        ],
        out_specs=pl.BlockSpec((tm, D), lambda i: (i, 0)),
        out_shape=jax.ShapeDtypeStruct((R, D), F32),
        compiler_params=pltpu.CompilerParams(
            dimension_semantics=("parallel",), vmem_limit_bytes=VMEM_LIMIT),
        name="ffn_down",
    )(act, wd, x1, gf)


def _t5_bucket(rel):
    half = T5_BUCKETS // 2
    max_exact = half // 2
    n = np.abs(rel)
    large = max_exact + (np.log(np.maximum(n, max_exact) / max_exact)
                         / np.log(T5_MAX_DIST / max_exact) * (half - max_exact)).astype(np.int32)
    large = np.minimum(large, half - 1)
    return (rel > 0).astype(np.int32) * half + np.where(n < max_exact, n, large).astype(np.int32)


def _toeplitz(e, nq):
    length = e.shape[-1]
    f = jnp.concatenate([e, jnp.zeros(e.shape[:-1] + (1,), e.dtype)], axis=-1)
    flat = jnp.tile(f, (1,) * (e.ndim - 1) + (nq,))[..., :nq * length]
    return flat.reshape(e.shape[:-1] + (nq, length))[..., nq - 1:]


_DIL_GEO = {16: (128, 384, 1, 1), 4: (64, 128, 2, 2), 1: (32, 64, RES, RES)}
_EXT = 2048


def _permute_cols(t, perm):
    n = len(perm)
    onehot = np.zeros((n, n), np.float32)
    onehot[perm, np.arange(n)] = 1.0
    return jnp.einsum("...k,kj->...j", t, onehot, precision=lax.Precision.HIGHEST)


def _dil_table(bias, dil):
    qm, km, nres, stride = _DIL_GEO[dil]
    heads = bias.shape[0]
    if dil == 16:
        band = jnp.stack([bias, jnp.full_like(bias, NEG)], axis=-1).reshape(heads, -1)[:, :4 * RADIUS + 1]
    else:
        band = bias
    half = band.shape[-1] // 2
    neg = jnp.full((heads, _EXT - half), NEG, F32)
    ext = jnp.concatenate([neg, band, neg], axis=-1)
    nq, nk = qm * stride, km * stride
    halo = (km - qm) // 2
    tabs = []
    for off in (-halo, 0, -2 * halo):
        lo = stride * off - (nq - 1) + _EXT
        tabs.append(_toeplitz(ext[:, lo:lo + nq + nk - 1], nq))
    t = jnp.stack(tabs, axis=1)
    if nres > 1:
        t = jnp.transpose(t.reshape(heads, 3, qm, nres, nk), (0, 1, 3, 2, 4)).reshape(heads, 3, nq, nk)
        final = np.arange(nk)
        t = _permute_cols(t, (final % km) * nres + final // km)
    return t


def _dil_tables(t5_table):
    out = []
    for _, dil in DIL_PAIRS:
        bias = t5_table[_t5_bucket(dil * np.arange(-RADIUS, RADIUS + 1))].astype(F32)
        out.append(_dil_table(bias.T * LOG2E, dil))
    return out


def _na_table(rpb, rows):
    npairs = rows // 2
    heads = rpb.shape[-1]
    c = np.arange(GRID_W)
    cs = np.clip(c - NA_COLS // 2, 0, GRID_W - NA_COLS)
    col_ok = (c[None, :] >= cs[:, None]) & (c[None, :] < cs[:, None] + NA_COLS)
    pad = GRID_W - NA_COLS
    e = jnp.pad(jnp.transpose(rpb, (2, 0, 1)).astype(F32) * LOG2E, ((0, 0), (0, 0), (pad, pad)))
    cmat = jnp.where(col_ok, _toeplitz(e, GRID_W), NEG)
    dr_idx, row_ok = [], []
    for r2 in (2, 0, 1, npairs - 2, npairs - 1):
        w0 = int(np.clip(2 * r2 - 4, 0, rows - NA_WIN_ROWS))
        rq = 2 * r2 + np.arange(2)
        rk = w0 + np.arange(NA_WIN_ROWS)
        rs = np.clip(rq - NA_ROWS // 2, 0, rows - NA_ROWS)
        row_ok.append((rk[None, :] >= rs[:, None]) & (rk[None, :] < rs[:, None] + NA_ROWS))
        dr_idx.append(np.clip(rk[None, :] - rq[:, None] + NA_ROWS - 1, 0, 2 * NA_ROWS - 2))
    g = jnp.take(cmat, np.stack(dr_idx), axis=1)
    g = jnp.where(np.stack(row_ok)[..., None, None], g, NEG)
    sub = GRID_W // RES
    nk = NA_WIN_ROWS * GRID_W
    g = jnp.transpose(g, (0, 1, 2, 4, 3, 5)).reshape(heads, 5, 2, sub, RES, nk)
    g = jnp.transpose(g, (0, 1, 4, 2, 3, 5)).reshape(heads, 5, RES * NA_SUB_M, nk)
    final = np.arange(nk)
    rho, rest = np.divmod(final, NA_WIN_M)
    return _permute_cols(g, (rest // sub) * GRID_W + (rest % sub) * RES + rho)


def _na_kernel(q_ref, k_ref, v_ref, t_ref, o_ref, *, rows):
    npairs = rows // 2
    subs = TILE_M // NA_SUB_M
    tile = pl.program_id(2)

    def window(j):
        r2 = tile * subs + j
        w0 = jnp.clip(2 * r2 - 4, 0, rows - NA_WIN_ROWS)
        var = jnp.where(r2 == 0, 1, jnp.where(r2 == 1, 2, jnp.where(
            r2 == npairs - 2, 3, jnp.where(r2 == npairs - 1, 4, 0))))
        return pl.multiple_of(w0 * (GRID_W // RES), 16), var

    def scores(j):
        ws, var = window(j)
        q = jnp.concatenate([q_ref[r, pl.ds(j * NA_SUB_M, NA_SUB_M), :] for r in range(RES)], axis=0)
        k = jnp.concatenate([k_ref[r, pl.ds(ws, NA_WIN_M), :] for r in range(RES)], axis=0)
        return _dot_nt(q, k) + t_ref[var]

    pending = [scores(j) for j in range(NA_AHEAD)]
    for j in range(subs):
        if j + NA_AHEAD < subs:
            pending.append(scores(j + NA_AHEAD))
        s = pending.pop(0)
        ws, _ = window(j)
        v = jnp.concatenate([v_ref[r, pl.ds(ws, NA_WIN_M), :] for r in range(RES)], axis=0)
        m = jnp.max(s, axis=-1, keepdims=True)
        p = jnp.exp2(s - m)
        l = jnp.sum(p, axis=-1, keepdims=True)
        o = jnp.dot(p.astype(BF16), v, preferred_element_type=F32) / l
        for r in range(RES):
            o_ref[r, pl.ds(j * NA_SUB_M, NA_SUB_M), :] = o[r * NA_SUB_M:(r + 1) * NA_SUB_M].astype(BF16)


def _na_call(qkv, table, rows):
    _, B, _, M, _ = qkv.shape
    blk_q = (None, None, RES, TILE_M, HEAD_DIM)
    blk_kv = (None, None, RES, M, HEAD_DIM)
    return pl.pallas_call(
        functools.partial(_na_kernel, rows=rows),
        grid=(B, H_NA, M // TILE_M),
        in_specs=[
            pl.BlockSpec(blk_q, lambda b, h, i: (h, b, 0, i, 0)),
            pl.BlockSpec(blk_kv, lambda b, h, i: (H_NA + h, b, 0, 0, 0)),
            pl.BlockSpec(blk_kv, lambda b, h, i: (2 * H_NA + h, b, 0, 0, 0)),
            pl.BlockSpec((None,) + table.shape[1:], lambda b, h, i: (h, 0, 0, 0)),
        ],
        out_specs=pl.BlockSpec(blk_q, lambda b, h, i: (h, b, 0, i, 0)),
        out_shape=jax.ShapeDtypeStruct((H_NA, B, RES, M, HEAD_DIM), BF16),
        compiler_params=pltpu.CompilerParams(
            dimension_semantics=("parallel", "parallel", "arbitrary"), vmem_limit_bytes=VMEM_LIMIT),
        name="na_attn",
    )(qkv, qkv, qkv, table)


def _edge_variant(blk, nblk):
    return jnp.where(blk == 0, 1, jnp.where(blk == nblk - 1, 2, 0))


def _flash_step(s, v, old):
    tiles = [s[:, c:c + HEAD_DIM] for c in range(0, s.shape[1], HEAD_DIM)]
    m_new = functools.reduce(jnp.maximum, tiles)
    m_new = jnp.broadcast_to(jnp.max(m_new, axis=-1, keepdims=True), m_new.shape)
    if old is not None:
        m_old, l_old, acc_old = old
        m_new = jnp.maximum(m_old, m_new)
        alpha = jnp.exp2(m_old - m_new)
    p = [jnp.exp2(t - m_new) for t in tiles]
    l_new = functools.reduce(jnp.add, p)
    acc = jnp.dot(jnp.concatenate(p, axis=1).astype(BF16), v, preferred_element_type=F32)
    if old is not None:
        l_new = alpha * l_old + l_new
        acc = alpha * acc_old + acc
    return m_new, l_new, acc


def _rows(ref, rr, start, size):
    return jnp.concatenate([ref[r, pl.ds(start, size), :] for r in rr], axis=0)


def _dil_kernel(q_ref, k_ref, v_ref, t1_ref, t4_ref, t16_ref, o_ref, m_scr, l_scr, acc_scr, *, m_total):
    tile = pl.program_id(2)
    stats = (m_scr, l_scr, acc_scr)

    blocks = []
    for dil, t_ref in ((16, t16_ref), (4, t4_ref), (1, t1_ref)):
        qm, km, nres, _ = _DIL_GEO[dil]
        for sub in range(TILE_M // qm):
            for r0 in range(RES // nres):
                blocks.append((t_ref, qm, km, sub, tuple(range(r0, RES, RES // nres)), dil == 16))

    def window(blk):
        _, qm, km, sub, _, _ = blk
        idx = tile * (TILE_M // qm) + sub
        ws = pl.multiple_of(jnp.clip(idx * qm - (km - qm) // 2, 0, m_total - km), 16)
        return ws, _edge_variant(idx, m_total // qm)

    def scores(blk):
        t_ref, qm, km, sub, rr, _ = blk
        ws, var = window(blk)
        return _dot_nt(_rows(q_ref, rr, sub * qm, qm), _rows(k_ref, rr, ws, km)) + t_ref[var]

    def finish(blk, s):
        _, qm, km, sub, rr, first = blk
        ws, _ = window(blk)
        old = None if first else tuple(_rows(ref, rr, sub * qm, qm) for ref in stats)
        new = _flash_step(s, _rows(v_ref, rr, ws, km), old)
        for ref, val in zip(stats, new):
            for a, r in enumerate(rr):
                ref[r, pl.ds(sub * qm, qm), :] = val[a * qm:(a + 1) * qm]

    pending = [scores(blk) for blk in blocks[:DIL_AHEAD]]
    for n, blk in enumerate(blocks):
        if n + DIL_AHEAD < len(blocks):
            pending.append(scores(blocks[n + DIL_AHEAD]))
        finish(blk, pending.pop(0))

    o_ref[...] = (acc_scr[...] / jnp.sum(l_scr[...], axis=-1, keepdims=True)).astype(BF16)


def _dil_call(qkv, t1, t4, t16):
    _, B, _, M, _ = qkv.shape
    base = 3 * H_NA
    blk_q = (None, None, RES, TILE_M, HEAD_DIM)
    blk_kv = (None, None, RES, M, HEAD_DIM)
    tspec = lambda t: pl.BlockSpec((None,) + t.shape[1:], lambda b, h, i: (h, 0, 0, 0))
    stat = pltpu.VMEM((RES, TILE_M, HEAD_DIM), F32)
    return pl.pallas_call(
        functools.partial(_dil_kernel, m_total=M),
        grid=(B, H_DIL, M // TILE_M),
        in_specs=[
            pl.BlockSpec(blk_q, lambda b, h, i: (base + h, b, 0, i, 0)),
            pl.BlockSpec(blk_kv, lambda b, h, i: (base + H_DIL + h, b, 0, 0, 0)),
            pl.BlockSpec(blk_kv, lambda b, h, i: (base + 2 * H_DIL + h, b, 0, 0, 0)),
            tspec(t1), tspec(t4), tspec(t16),
        ],
        out_specs=pl.BlockSpec(blk_q, lambda b, h, i: (h, b, 0, i, 0)),
        out_shape=jax.ShapeDtypeStruct((H_DIL, B, RES, M, HEAD_DIM), BF16),
        scratch_shapes=[stat, stat, stat],
        compiler_params=pltpu.CompilerParams(
            dimension_semantics=("parallel", "parallel", "arbitrary"), vmem_limit_bytes=VMEM_LIMIT),
        name="dil_attn",
    )(qkv, qkv, qkv, t1, t4, t16)


def _trunk(x, w_in, w_out, g_attn, g_na, g_dil, na_tables, dil_tables, g_ffn, w_gate, w_up, w_down,
           g_final, q_scale):
    B, S, D = x.shape
    M = S // RES
    R = B * S
    depth = w_in.shape[0]
    assert S % (RES * TILE_M) == 0 and M >= 512 and S % (2 * GRID_W) == 0
    rows = S // GRID_W
    x8 = jnp.transpose(x.reshape(B, M, RES, D), (0, 2, 1, 3)).reshape(R, D)
    for l in range(depth):
        qkv = _qkv_call(x8, g_attn[l][None], w_in, q_scale, layer=l, tm=QKV_TM, tn=QKV_TN)
        qkv = qkv.reshape(qkv.shape[0], B, RES, M, HEAD_DIM)
        oa = _na_call(qkv, na_tables[(l, rows)], rows).reshape(H_NA, R, HEAD_DIM)
        ob = _dil_call(qkv, *dil_tables).reshape(H_DIL, R, HEAD_DIM)
        x1, h = _outproj_call(oa, ob, g_na[l][None], g_dil[l][None], w_out, x8, g_ffn[l][None],
                              layer=l, tm=OUTPROJ_TM)
        act = _ffn_up_call(h, w_gate, w_up, layer=l, tm=FFN_UP_TM, tf=FFN_UP_TF)
        x8 = _ffn_down_call(act, w_down, x1, g_final[None], layer=l, tm=FFN_DOWN_TM,
                            final_norm=(l == depth - 1))
    return jnp.transpose(x8.reshape(B, RES, M, D), (0, 2, 1, 3)).reshape(B, S, D)


def kernel(x_prompt, x_sample, w_in, w_out, g_attn, g_na, g_dil, rpb_na, t5_table, g_ffn, w_gate, w_up,
           w_down, g_final):
    depth = w_in.shape[0]
    w_na = H_NA * HEAD_DIM
    w_dil = H_DIL * HEAD_DIM
    col = np.arange(3 * (w_na + w_dil))
    is_q = (col < w_na) | ((col >= 3 * w_na) & (col < 3 * w_na + w_dil))
    q_scale = jnp.asarray(np.where(is_q, SCALE * LOG2E, 1.0)[None], F32)
    dil_tables = _dil_tables(t5_table)
    na_tables = {}
    for x in (x_prompt, x_sample):
        rows = x.shape[1] // GRID_W
        for l in range(depth):
            if (l, rows) not in na_tables:
                na_tables[(l, rows)] = _na_table(rpb_na[l], rows)
    wb = [w.astype(BF16) for w in (w_in, w_out, w_gate, w_up, w_down)]
    outs = []
    for x in (x_prompt, x_sample):
        outs.append(_trunk(x, wb[0], wb[1], g_attn, g_na, g_dil, na_tables, dil_tables, g_ffn,
                           wb[2], wb[3], wb[4], g_final, q_scale))
    return tuple(outs)
```

```python
import functools
import math

import jax
import jax.numpy as jnp
import numpy as np
from jax import lax
from jax.experimental import pallas as pl
from jax.experimental.pallas import tpu as pltpu

HEAD_DIM = 128
H_NA = 8
H_DIL = 8
GRID_W = 64
NA_ROWS = 8
NA_COLS = 16
DIL_PAIRS = ((128, 1), (512, 4), (2048, 16))
T5_BUCKETS = 32
T5_MAX_DIST = 2048
EPS = 1e-6
NEG = -1e30
SCALE = 1.0 / math.sqrt(HEAD_DIM)
LOG2E = math.log2(math.e)

RES = 8
RADIUS = 64
TILE_M = 512
NA_SUB_M = 16
NA_WIN_ROWS = 10
NA_WIN_M = NA_WIN_ROWS * GRID_W // RES
DIL_AHEAD = 5
NA_AHEAD = 3
QKV_TM, QKV_TN = 1024, 2048
QKV_CHUNKS = 4
OUTPROJ_TM = 512
FFN_UP_TM, FFN_UP_TF = 4096, 512
FFN_UP_CHUNKS = 16
FFN_DOWN_TM = 512
VMEM_LIMIT = 56 * 1024 * 1024

BF16 = jnp.bfloat16
F32 = jnp.float32


def _dot_nt(a, b):
    return lax.dot_general(a, b, (((1,), (1,)), ((), ())), preferred_element_type=F32)


def _rmsnorm_rows(x, g):
    ms = jnp.mean(x * x, axis=-1, keepdims=True)
    return x * lax.rsqrt(ms + EPS) * g


def _qkv_kernel(x_ref, g_ref, w_ref, sc_ref, o_ref, h_scr):
    rows = x_ref.shape[0] // QKV_CHUNKS

    def project(k):
        r = pl.ds(k * rows, rows)
        acc = jnp.dot(h_scr[r, :], w_ref[...], preferred_element_type=F32) * sc_ref[...]
        for c in range(o_ref.shape[0]):
            o_ref[c, r, :] = acc[:, c * HEAD_DIM:(c + 1) * HEAD_DIM].astype(BF16)

    @pl.when(pl.program_id(1) == 0)
    def _():
        for k in range(QKV_CHUNKS):
            r = pl.ds(k * rows, rows)
            h_scr[r, :] = _rmsnorm_rows(x_ref[r, :], g_ref[...]).astype(BF16)
            project(k)

    @pl.when(pl.program_id(1) != 0)
    def _():
        for k in range(QKV_CHUNKS):
            project(k)


def _qkv_call(x, g, w, sc, layer, tm, tn):
    R, D = x.shape
    N = w.shape[2]
    return pl.pallas_call(
        _qkv_kernel,
        grid=(R // tm, N // tn),
        in_specs=[
            pl.BlockSpec((tm, D), lambda i, j: (i, 0)),
            pl.BlockSpec((1, D), lambda i, j: (0, 0)),
            pl.BlockSpec((None, D, tn), lambda i, j: (layer, 0, j)),
            pl.BlockSpec((1, tn), lambda i, j: (0, j)),
        ],
        out_specs=pl.BlockSpec((tn // HEAD_DIM, tm, HEAD_DIM), lambda i, j: (j, i, 0)),
        out_shape=jax.ShapeDtypeStruct((N // HEAD_DIM, R, HEAD_DIM), BF16),
        scratch_shapes=[pltpu.VMEM((tm, D), BF16)],
        compiler_params=pltpu.CompilerParams(
            dimension_semantics=("parallel", "arbitrary"), vmem_limit_bytes=VMEM_LIMIT),
        name="qkv_proj",
    )(x, g, w, sc)


def _group_norm(ref, g):
    a = jnp.concatenate([ref[c] for c in range(ref.shape[0])], axis=1).astype(F32)
    return _rmsnorm_rows(a, g).astype(BF16)


def _outproj_kernel(oa_ref, ob_ref, ga_ref, gb_ref, wo_ref, x_ref, g_ref, x1_ref, h_ref):
    mix = jnp.concatenate([_group_norm(oa_ref, ga_ref[...]), _group_norm(ob_ref, gb_ref[...])], axis=1)
    x = x_ref[...] + jnp.dot(mix, wo_ref[...], preferred_element_type=F32)
    x1_ref[...] = x
    h_ref[...] = _rmsnorm_rows(x, g_ref[...]).astype(BF16)


def _outproj_call(oa, ob, ga, gb, wo, x, g, layer, tm):
    R, D = x.shape
    ha, hb = oa.shape[0], ob.shape[0]
    row = lambda i: (0, 0)
    return pl.pallas_call(
        _outproj_kernel,
        grid=(R // tm,),
        in_specs=[
            pl.BlockSpec((ha, tm, HEAD_DIM), lambda i: (0, i, 0)),
            pl.BlockSpec((hb, tm, HEAD_DIM), lambda i: (0, i, 0)),
            pl.BlockSpec((1, ha * HEAD_DIM), row),
            pl.BlockSpec((1, hb * HEAD_DIM), row),
            pl.BlockSpec((None,) + wo.shape[1:], lambda i: (layer, 0, 0), pipeline_mode=pl.Buffered(1)),
            pl.BlockSpec((tm, D), lambda i: (i, 0)),
            pl.BlockSpec((1, D), row),
        ],
        out_specs=[pl.BlockSpec((tm, D), lambda i: (i, 0)), pl.BlockSpec((tm, D), lambda i: (i, 0))],
        out_shape=[jax.ShapeDtypeStruct((R, D), F32), jax.ShapeDtypeStruct((R, D), BF16)],
        compiler_params=pltpu.CompilerParams(
            dimension_semantics=("parallel",), vmem_limit_bytes=VMEM_LIMIT),
        name="out_proj",
    )(oa, ob, ga, gb, wo, x, g)


def _ffn_up_kernel(h_ref, wg_ref, wu_ref, a_ref):
    rows = h_ref.shape[0] // FFN_UP_CHUNKS
    for k in range(FFN_UP_CHUNKS):
        h = h_ref[k * rows:(k + 1) * rows]
        gate = jnp.dot(h, wg_ref[...], preferred_element_type=F32)
        up = jnp.dot(h, wu_ref[...], preferred_element_type=F32)
        a_ref[k * rows:(k + 1) * rows] = (gate * (1.0 / (1.0 + jnp.exp(-gate))) * up).astype(BF16)


def _ffn_up_call(h, wg, wu, layer, tm, tf):
    R, D = h.shape
    F = wg.shape[2]
    return pl.pallas_call(
        _ffn_up_kernel,
        grid=(R // tm, F // tf),
        in_specs=[
            pl.BlockSpec((tm, D), lambda i, f: (i, 0)),
            pl.BlockSpec((None, D, tf), lambda i, f: (layer, 0, f)),
            pl.BlockSpec((None, D, tf), lambda i, f: (layer, 0, f)),
        ],
        out_specs=pl.BlockSpec((tm, tf), lambda i, f: (i, f)),
        out_shape=jax.ShapeDtypeStruct((R, F), BF16),
        compiler_params=pltpu.CompilerParams(
            dimension_semantics=("parallel", "arbitrary"), vmem_limit_bytes=VMEM_LIMIT),
        name="ffn_up",
    )(h, wg, wu)


def _ffn_down_kernel(a_ref, wd_ref, x_ref, gf_ref, o_ref, *, final_norm):
    y = x_ref[...] + jnp.dot(a_ref[...], wd_ref[...], preferred_element_type=F32)
    if final_norm:
        y = _rmsnorm_rows(y, gf_ref[...])
    o_ref[...] = y


def _ffn_down_call(act, wd, x1, gf, layer, tm, final_norm):
    R, F = act.shape
    D = x1.shape[1]
    return pl.pallas_call(
        functools.partial(_ffn_down_kernel, final_norm=final_norm),
        grid=(R // tm,),
        in_specs=[
            pl.BlockSpec((tm, F), lambda i: (i, 0)),
            pl.BlockSpec((None, F, D), lambda i: (layer, 0, 0), pipeline_mode=pl.Buffered(1)),
            pl.BlockSpec((tm, D), lambda i: (i, 0)),
            pl.BlockSpec((1, D), lambda i: (0, 0)),
        ],
        out_specs=pl.BlockSpec((tm, D), lambda i: (i, 0)),
        out_shape=jax.ShapeDtypeStruct((R, D), F32),
        compiler_params=pltpu.CompilerParams(
            dimension_semantics=("parallel",), vmem_limit_bytes=VMEM_LIMIT),
        name="ffn_down",
    )(act, wd, x1, gf)


def _t5_bucket(rel):
    half = T5_BUCKETS // 2
    max_exact = half // 2
    n = np.abs(rel)
    large = max_exact + (np.log(np.maximum(n, max_exact) / max_exact)
                         / np.log(T5_MAX_DIST / max_exact) * (half - max_exact)).astype(np.int32)
    large = np.minimum(large, half - 1)
    return (rel > 0).astype(np.int32) * half + np.where(n < max_exact, n, large).astype(np.int32)


def _toeplitz(e, nq):
    length = e.shape[-1]
    f = jnp.concatenate([e, jnp.zeros(e.shape[:-1] + (1,), e.dtype)], axis=-1)
    flat = jnp.tile(f, (1,) * (e.ndim - 1) + (nq,))[..., :nq * length]
    return flat.reshape(e.shape[:-1] + (nq, length))[..., nq - 1:]


_DIL_GEO = {16: (128, 384, 1, 1), 4: (64, 128, 2, 2), 1: (32, 64, RES, RES)}
_EXT = 2048


def _permute_cols(t, perm):
    n = len(perm)
    onehot = np.zeros((n, n), np.float32)
    onehot[perm, np.arange(n)] = 1.0
    return jnp.einsum("...k,kj->...j", t, onehot, precision=lax.Precision.HIGHEST)


def _dil_table(bias, dil):
    qm, km, nres, stride = _DIL_GEO[dil]
    heads = bias.shape[0]
    if dil == 16:
        band = jnp.stack([bias, jnp.full_like(bias, NEG)], axis=-1).reshape(heads, -1)[:, :4 * RADIUS + 1]
    else:
        band = bias
    half = band.shape[-1] // 2
    neg = jnp.full((heads, _EXT - half), NEG, F32)
    ext = jnp.concatenate([neg, band, neg], axis=-1)
    nq, nk = qm * stride, km * stride
    halo = (km - qm) // 2
    tabs = []
    for off in (-halo, 0, -2 * halo):
        lo = stride * off - (nq - 1) + _EXT
        tabs.append(_toeplitz(ext[:, lo:lo + nq + nk - 1], nq))
    t = jnp.stack(tabs, axis=1)
    if nres > 1:
        t = jnp.transpose(t.reshape(heads, 3, qm, nres, nk), (0, 1, 3, 2, 4)).reshape(heads, 3, nq, nk)
        final = np.arange(nk)
        t = _permute_cols(t, (final % km) * nres + final // km)
    return t


def _dil_tables(t5_table):
    out = []
    for _, dil in DIL_PAIRS:
        bias = t5_table[_t5_bucket(dil * np.arange(-RADIUS, RADIUS + 1))].astype(F32)
        out.append(_dil_table(bias.T * LOG2E, dil))
    return out


def _na_table(rpb, rows):
    npairs = rows // 2
    heads = rpb.shape[-1]
    c = np.arange(GRID_W)
    cs = np.clip(c - NA_COLS // 2, 0, GRID_W - NA_COLS)
    col_ok = (c[None, :] >= cs[:, None]) & (c[None, :] < cs[:, None] + NA_COLS)
    pad = GRID_W - NA_COLS
    e = jnp.pad(jnp.transpose(rpb, (2, 0, 1)).astype(F32) * LOG2E, ((0, 0), (0, 0), (pad, pad)))
    cmat = jnp.where(col_ok, _toeplitz(e, GRID_W), NEG)
    dr_idx, row_ok = [], []
    for r2 in (2, 0, 1, npairs - 2, npairs - 1):
        w0 = int(np.clip(2 * r2 - 4, 0, rows - NA_WIN_ROWS))
        rq = 2 * r2 + np.arange(2)
        rk = w0 + np.arange(NA_WIN_ROWS)
        rs = np.clip(rq - NA_ROWS // 2, 0, rows - NA_ROWS)
        row_ok.append((rk[None, :] >= rs[:, None]) & (rk[None, :] < rs[:, None] + NA_ROWS))
        dr_idx.append(np.clip(rk[None, :] - rq[:, None] + NA_ROWS - 1, 0, 2 * NA_ROWS - 2))
    g = jnp.take(cmat, np.stack(dr_idx), axis=1)
    g = jnp.where(np.stack(row_ok)[..., None, None], g, NEG)
    sub = GRID_W // RES
    nk = NA_WIN_ROWS * GRID_W
    g = jnp.transpose(g, (0, 1, 2, 4, 3, 5)).reshape(heads, 5, 2, sub, RES, nk)
    g = jnp.transpose(g, (0, 1, 4, 2, 3, 5)).reshape(heads, 5, RES * NA_SUB_M, nk)
    final = np.arange(nk)
    rho, rest = np.divmod(final, NA_WIN_M)
    return _permute_cols(g, (rest // sub) * GRID_W + (rest % sub) * RES + rho)


def _na_kernel(q_ref, k_ref, v_ref, t_ref, o_ref, *, rows):
    npairs = rows // 2
    subs = TILE_M // NA_SUB_M
    tile = pl.program_id(2)

    def window(j):
        r2 = tile * subs + j
        w0 = jnp.clip(2 * r2 - 4, 0, rows - NA_WIN_ROWS)
        var = jnp.where(r2 == 0, 1, jnp.where(r2 == 1, 2, jnp.where(
            r2 == npairs - 2, 3, jnp.where(r2 == npairs - 1, 4, 0))))
        return pl.multiple_of(w0 * (GRID_W // RES), 16), var

    def scores(j):
        ws, var = window(j)
        q = jnp.concatenate([q_ref[r, pl.ds(j * NA_SUB_M, NA_SUB_M), :] for r in range(RES)], axis=0)
        k = jnp.concatenate([k_ref[r, pl.ds(ws, NA_WIN_M), :] for r in range(RES)], axis=0)
        return _dot_nt(q, k) + t_ref[var]

    pending = [scores(j) for j in range(NA_AHEAD)]
    for j in range(subs):
        if j + NA_AHEAD < subs:
            pending.append(scores(j + NA_AHEAD))
        s = pending.pop(0)
        ws, _ = window(j)
        v = jnp.concatenate([v_ref[r, pl.ds(ws, NA_WIN_M), :] for r in range(RES)], axis=0)
        m = jnp.max(s, axis=-1, keepdims=True)
        p = jnp.exp2(s - m)
        l = jnp.sum(p, axis=-1, keepdims=True)
        o = jnp.dot(p.astype(BF16), v, preferred_element_type=F32) / l
        for r in range(RES):
            o_ref[r, pl.ds(j * NA_SUB_M, NA_SUB_M), :] = o[r * NA_SUB_M:(r + 1) * NA_SUB_M].astype(BF16)


def _na_call(qkv, table, rows):
    _, B, _, M, _ = qkv.shape
    blk_q = (None, None, RES, TILE_M, HEAD_DIM)
    blk_kv = (None, None, RES, M, HEAD_DIM)
    return pl.pallas_call(
        functools.partial(_na_kernel, rows=rows),
        grid=(B, H_NA, M // TILE_M),
        in_specs=[
            pl.BlockSpec(blk_q, lambda b, h, i: (h, b, 0, i, 0)),
            pl.BlockSpec(blk_kv, lambda b, h, i: (H_NA + h, b, 0, 0, 0)),
            pl.BlockSpec(blk_kv, lambda b, h, i: (2 * H_NA + h, b, 0, 0, 0)),
            pl.BlockSpec((None,) + table.shape[1:], lambda b, h, i: (h, 0, 0, 0)),
        ],
        out_specs=pl.BlockSpec(blk_q, lambda b, h, i: (h, b, 0, i, 0)),
        out_shape=jax.ShapeDtypeStruct((H_NA, B, RES, M, HEAD_DIM), BF16),
        compiler_params=pltpu.CompilerParams(
            dimension_semantics=("parallel", "parallel", "arbitrary"), vmem_limit_bytes=VMEM_LIMIT),
        name="na_attn",
    )(qkv, qkv, qkv, table)


def _edge_variant(blk, nblk):
    return jnp.where(blk == 0, 1, jnp.where(blk == nblk - 1, 2, 0))


def _flash_step(s, v, old):
    tiles = [s[:, c:c + HEAD_DIM] for c in range(0, s.shape[1], HEAD_DIM)]
    m_new = functools.reduce(jnp.maximum, tiles)
    m_new = jnp.broadcast_to(jnp.max(m_new, axis=-1, keepdims=True), m_new.shape)
    if old is not None:
        m_old, l_old, acc_old = old
        m_new = jnp.maximum(m_old, m_new)
        alpha = jnp.exp2(m_old - m_new)
    p = [jnp.exp2(t - m_new) for t in tiles]
    l_new = functools.reduce(jnp.add, p)
    acc = jnp.dot(jnp.concatenate(p, axis=1).astype(BF16), v, preferred_element_type=F32)
    if old is not None:
        l_new = alpha * l_old + l_new
        acc = alpha * acc_old + acc
    return m_new, l_new, acc


def _rows(ref, rr, start, size):
    return jnp.concatenate([ref[r, pl.ds(start, size), :] for r in rr], axis=0)


def _dil_kernel(q_ref, k_ref, v_ref, t1_ref, t4_ref, t16_ref, o_ref, m_scr, l_scr, acc_scr, *, m_total):
    tile = pl.program_id(2)
    stats = (m_scr, l_scr, acc_scr)

    blocks = []
    for dil, t_ref in ((16, t16_ref), (4, t4_ref), (1, t1_ref)):
        qm, km, nres, _ = _DIL_GEO[dil]
        for sub in range(TILE_M // qm):
            for r0 in range(RES // nres):
                blocks.append((t_ref, qm, km, sub, tuple(range(r0, RES, RES // nres)), dil == 16))

    def window(blk):
        _, qm, km, sub, _, _ = blk
        idx = tile * (TILE_M // qm) + sub
        ws = pl.multiple_of(jnp.clip(idx * qm - (km - qm) // 2, 0, m_total - km), 16)
        return ws, _edge_variant(idx, m_total // qm)

    def scores(blk):
        t_ref, qm, km, sub, rr, _ = blk
        ws, var = window(blk)
        return _dot_nt(_rows(q_ref, rr, sub * qm, qm), _rows(k_ref, rr, ws, km)) + t_ref[var]

    def finish(blk, s):
        _, qm, km, sub, rr, first = blk
        ws, _ = window(blk)
        old = None if first else tuple(_rows(ref, rr, sub * qm, qm) for ref in stats)
        new = _flash_step(s, _rows(v_ref, rr, ws, km), old)
        for ref, val in zip(stats, new):
            for a, r in enumerate(rr):
                ref[r, pl.ds(sub * qm, qm), :] = val[a * qm:(a + 1) * qm]

    pending = [scores(blk) for blk in blocks[:DIL_AHEAD]]
    for n, blk in enumerate(blocks):
        if n + DIL_AHEAD < len(blocks):
            pending.append(scores(blocks[n + DIL_AHEAD]))
        finish(blk, pending.pop(0))

    o_ref[...] = (acc_scr[...] / jnp.sum(l_scr[...], axis=-1, keepdims=True)).astype(BF16)


def _dil_call(qkv, t1, t4, t16):
    _, B, _, M, _ = qkv.shape
    base = 3 * H_NA
    blk_q = (None, None, RES, TILE_M, HEAD_DIM)
    blk_kv = (None, None, RES, M, HEAD_DIM)
    tspec = lambda t: pl.BlockSpec((None,) + t.shape[1:], lambda b, h, i: (h, 0, 0, 0))
    stat = pltpu.VMEM((RES, TILE_M, HEAD_DIM), F32)
    return pl.pallas_call(
        functools.partial(_dil_kernel, m_total=M),
        grid=(B, H_DIL, M // TILE_M),
        in_specs=[
            pl.BlockSpec(blk_q, lambda b, h, i: (base + h, b, 0, i, 0)),
            pl.BlockSpec(blk_kv, lambda b, h, i: (base + H_DIL + h, b, 0, 0, 0)),
            pl.BlockSpec(blk_kv, lambda b, h, i: (base + 2 * H_DIL + h, b, 0, 0, 0)),
            tspec(t1), tspec(t4), tspec(t16),
        ],
        out_specs=pl.BlockSpec(blk_q, lambda b, h, i: (h, b, 0, i, 0)),
        out_shape=jax.ShapeDtypeStruct((H_DIL, B, RES, M, HEAD_DIM), BF16),
        scratch_shapes=[stat, stat, stat],
        compiler_params=pltpu.CompilerParams(
            dimension_semantics=("parallel", "parallel", "arbitrary"), vmem_limit_bytes=VMEM_LIMIT),
        name="dil_attn",
    )(qkv, qkv, qkv, t1, t4, t16)


def _trunk(x, w_in, w_out, g_attn, g_na, g_dil, na_tables, dil_tables, g_ffn, w_gate, w_up, w_down,
           g_final, q_scale):
    B, S, D = x.shape
    M = S // RES
    R = B * S
    depth = w_in.shape[0]
    assert S % (RES * TILE_M) == 0 and M >= 512 and S % (2 * GRID_W) == 0
    rows = S // GRID_W
    x8 = jnp.transpose(x.reshape(B, M, RES, D), (0, 2, 1, 3)).reshape(R, D)
    for l in range(depth):
        qkv = _qkv_call(x8, g_attn[l][None], w_in, q_scale, layer=l, tm=QKV_TM, tn=QKV_TN)
        qkv = qkv.reshape(qkv.shape[0], B, RES, M, HEAD_DIM)
        oa = _na_call(qkv, na_tables[(l, rows)], rows).reshape(H_NA, R, HEAD_DIM)
        ob = _dil_call(qkv, *dil_tables).reshape(H_DIL, R, HEAD_DIM)
        x1, h = _outproj_call(oa, ob, g_na[l][None], g_dil[l][None], w_out, x8, g_ffn[l][None],
                              layer=l, tm=OUTPROJ_TM)
        act = _ffn_up_call(h, w_gate, w_up, layer=l, tm=FFN_UP_TM, tf=FFN_UP_TF)
        x8 = _ffn_down_call(act, w_down, x1, g_final[None], layer=l, tm=FFN_DOWN_TM,
                            final_norm=(l == depth - 1))
    return jnp.transpose(x8.reshape(B, RES, M, D), (0, 2, 1, 3)).reshape(B, S, D)


def kernel(x_prompt, x_sample, w_in, w_out, g_attn, g_na, g_dil, rpb_na, t5_table, g_ffn, w_gate, w_up,
           w_down, g_final):
    depth = w_in.shape[0]
    w_na = H_NA * HEAD_DIM
    w_dil = H_DIL * HEAD_DIM
    col = np.arange(3 * (w_na + w_dil))
    is_q = (col < w_na) | ((col >= 3 * w_na) & (col < 3 * w_na + w_dil))
    q_scale = jnp.asarray(np.where(is_q, SCALE * LOG2E, 1.0)[None], F32)
    dil_tables = _dil_tables(t5_table)
    na_tables = {}
    for x in (x_prompt, x_sample):
        rows = x.shape[1] // GRID_W
        for l in range(depth):
            if (l, rows) not in na_tables:
                na_tables[(l, rows)] = _na_table(rpb_na[l], rows)
    wb = [w.astype(BF16) for w in (w_in, w_out, w_gate, w_up, w_down)]
    outs = []
    for x in (x_prompt, x_sample):
        outs.append(_trunk(x, wb[0], wb[1], g_attn, g_na, g_dil, na_tables, dil_tables, g_ffn,
                           wb[2], wb[3], wb[4], g_final, q_scale))
    return tuple(outs)
```

```python
import functools
import math

import jax
import jax.numpy as jnp
import numpy as np
from jax import lax
from jax.experimental import pallas as pl
from jax.experimental.pallas import tpu as pltpu

HEAD_DIM = 128
H_NA = 8
H_DIL = 8
GRID_W = 64
NA_ROWS = 8
NA_COLS = 16
DIL_PAIRS = ((128, 1), (512, 4), (2048, 16))
T5_BUCKETS = 32
T5_MAX_DIST = 2048
EPS = 1e-6
NEG = -1e30
SCALE = 1.0 / math.sqrt(HEAD_DIM)
LOG2E = math.log2(math.e)

RES = 8
RADIUS = 64
TILE_M = 512
NA_SUB_M = 16
NA_WIN_ROWS = 10
NA_WIN_M = NA_WIN_ROWS * GRID_W // RES
DIL_AHEAD = 5
NA_AHEAD = 3
QKV_TM, QKV_TN = 1024, 2048
QKV_CHUNKS = 4
OUTPROJ_TM = 512
FFN_UP_TM, FFN_UP_TF = 4096, 512
FFN_UP_CHUNKS = 16
FFN_DOWN_TM = 512
VMEM_LIMIT = 56 * 1024 * 1024

BF16 = jnp.bfloat16
F32 = jnp.float32


def _dot_nt(a, b):
    return lax.dot_general(a, b, (((1,), (1,)), ((), ())), preferred_element_type=F32)


def _rmsnorm_rows(x, g):
    ms = jnp.mean(x * x, axis=-1, keepdims=True)
    return x * lax.rsqrt(ms + EPS) * g


def _qkv_kernel(x_ref, g_ref, w_ref, sc_ref, o_ref, h_scr):
    rows = x_ref.shape[0] // QKV_CHUNKS

    def project(k):
        r = pl.ds(k * rows, rows)
        acc = jnp.dot(h_scr[r, :], w_ref[...], preferred_element_type=F32) * sc_ref[...]
        for c in range(o_ref.shape[0]):
            o_ref[c, r, :] = acc[:, c * HEAD_DIM:(c + 1) * HEAD_DIM].astype(BF16)

    @pl.when(pl.program_id(1) == 0)
    def _():
        for k in range(QKV_CHUNKS):
            r = pl.ds(k * rows, rows)
            h_scr[r, :] = _rmsnorm_rows(x_ref[r, :], g_ref[...]).astype(BF16)
            project(k)

    @pl.when(pl.program_id(1) != 0)
    def _():
        for k in range(QKV_CHUNKS):
            project(k)


def _qkv_call(x, g, w, sc, layer, tm, tn):
    R, D = x.shape
    N = w.shape[2]
    return pl.pallas_call(
        _qkv_kernel,
        grid=(R // tm, N // tn),
        in_specs=[
            pl.BlockSpec((tm, D), lambda i, j: (i, 0)),
            pl.BlockSpec((1, D), lambda i, j: (0, 0)),
            pl.BlockSpec((None, D, tn), lambda i, j: (layer, 0, j)),
            pl.BlockSpec((1, tn), lambda i, j: (0, j)),
        ],
        out_specs=pl.BlockSpec((tn // HEAD_DIM, tm, HEAD_DIM), lambda i, j: (j, i, 0)),
        out_shape=jax.ShapeDtypeStruct((N // HEAD_DIM, R, HEAD_DIM), BF16),
        scratch_shapes=[pltpu.VMEM((tm, D), BF16)],
        compiler_params=pltpu.CompilerParams(
            dimension_semantics=("parallel", "arbitrary"), vmem_limit_bytes=VMEM_LIMIT),
        name="qkv_proj",
    )(x, g, w, sc)


def _group_norm(ref, g):
    a = jnp.concatenate([ref[c] for c in range(ref.shape[0])], axis=1).astype(F32)
    return _rmsnorm_rows(a, g).astype(BF16)


def _outproj_kernel(oa_ref, ob_ref, ga_ref, gb_ref, wo_ref, x_ref, g_ref, x1_ref, h_ref):
    mix = jnp.concatenate([_group_norm(oa_ref, ga_ref[...]), _group_norm(ob_ref, gb_ref[...])], axis=1)
    x = x_ref[...] + jnp.dot(mix, wo_ref[...], preferred_element_type=F32)
    x1_ref[...] = x
    h_ref[...] = _rmsnorm_rows(x, g_ref[...]).astype(BF16)


def _outproj_call(oa, ob, ga, gb, wo, x, g, layer, tm):
    R, D = x.shape
    ha, hb = oa.shape[0], ob.shape[0]
    row = lambda i: (0, 0)
    return pl.pallas_call(
        _outproj_kernel,
        grid=(R // tm,),
        in_specs=[
            pl.BlockSpec((ha, tm, HEAD_DIM), lambda i: (0, i, 0)),
            pl.BlockSpec((hb, tm, HEAD_DIM), lambda i: (0, i, 0)),
            pl.BlockSpec((1, ha * HEAD_DIM), row),
            pl.BlockSpec((1, hb * HEAD_DIM), row),
            pl.BlockSpec((None,) + wo.shape[1:], lambda i: (layer, 0, 0), pipeline_mode=pl.Buffered(1)),
            pl.BlockSpec((tm, D), lambda i: (i, 0)),
            pl.BlockSpec((1, D), row),
        ],
        out_specs=[pl.BlockSpec((tm, D), lambda i: (i, 0)), pl.BlockSpec((tm, D), lambda i: (i, 0))],
        out_shape=[jax.ShapeDtypeStruct((R, D), F32), jax.ShapeDtypeStruct((R, D), BF16)],
        compiler_params=pltpu.CompilerParams(
            dimension_semantics=("parallel",), vmem_limit_bytes=VMEM_LIMIT),
        name="out_proj",
    )(oa, ob, ga, gb, wo, x, g)


def _ffn_up_kernel(h_ref, wg_ref, wu_ref, a_ref):
    rows = h_ref.shape[0] // FFN_UP_CHUNKS
    for k in range(FFN_UP_CHUNKS):
        h = h_ref[k * rows:(k + 1) * rows]
        gate = jnp.dot(h, wg_ref[...], preferred_element_type=F32)
        up = jnp.dot(h, wu_ref[...], preferred_element_type=F32)
        a_ref[k * rows:(k + 1) * rows] = (gate * (1.0 / (1.0 + jnp.exp(-gate))) * up).astype(BF16)


def _ffn_up_call(h, wg, wu, layer, tm, tf):
    R, D = h.shape
    F = wg.shape[2]
    return pl.pallas_call(
        _ffn_up_kernel,
        grid=(R // tm, F // tf),
        in_specs=[
            pl.BlockSpec((tm, D), lambda i, f: (i, 0)),
            pl.BlockSpec((None, D, tf), lambda i, f: (layer, 0, f)),
            pl.BlockSpec((None, D, tf), lambda i, f: (layer, 0, f)),
        ],
        out_specs=pl.BlockSpec((tm, tf), lambda i, f: (i, f)),
        out_shape=jax.ShapeDtypeStruct((R, F), BF16),
        compiler_params=pltpu.CompilerParams(
            dimension_semantics=("parallel", "arbitrary"), vmem_limit_bytes=VMEM_LIMIT),
        name="ffn_up",
    )(h, wg, wu)


def _ffn_down_kernel(a_ref, wd_ref, x_ref, gf_ref, o_ref, *, final_norm):
    y = x_ref[...] + jnp.dot(a_ref[...], wd_ref[...], preferred_element_type=F32)
    if final_norm:
        y = _rmsnorm_rows(y, gf_ref[...])
    o_ref[...] = y


def _ffn_down_call(act, wd, x1, gf, layer, tm, final_norm):
    R, F = act.shape
    D = x1.shape[1]
    return pl.pallas_call(
        functools.partial(_ffn_down_kernel, final_norm=final_norm),
        grid=(R // tm,),
        in_specs=[
            pl.BlockSpec((tm, F), lambda i: (i, 0)),
            pl.BlockSpec((None, F, D), lambda i: (layer, 0, 0), pipeline_mode=pl.Buffered(1)),
            pl.BlockSpec((tm, D), lambda i: (i, 0)),
            pl.BlockSpec((1, D), lambda i: (0, 0)),
        ],
        out_specs=pl.BlockSpec((tm, D), lambda i: (i, 0)),
        out_shape=jax.ShapeDtypeStruct((R, D), F32),
        compiler_params=pltpu.CompilerParams(
            dimension_semantics=("parallel",), vmem_limit_bytes=VMEM_LIMIT),
        name="ffn_down",
    )(act, wd, x1, gf)


def _t5_bucket(rel):
    half = T5_BUCKETS // 2
    max_exact = half // 2
    n = np.abs(rel)
    large = max_exact + (np.log(np.maximum(n, max_exact) / max_exact)
                         / np.log(T5_MAX_DIST / max_exact) * (half - max_exact)).astype(np.int32)
    large = np.minimum(large, half - 1)
    return (rel > 0).astype(np.int32) * half + np.where(n < max_exact, n, large).astype(np.int32)


def _toeplitz(e, nq):
    length = e.shape[-1]
    f = jnp.concatenate([e, jnp.zeros(e.shape[:-1] + (1,), e.dtype)], axis=-1)
    flat = jnp.tile(f, (1,) * (e.ndim - 1) + (nq,))[..., :nq * length]
    return flat.reshape(e.shape[:-1] + (nq, length))[..., nq - 1:]


_DIL_GEO = {16: (128, 384, 1, 1), 4: (64, 128, 2, 2), 1: (32, 64, RES, RES)}
_EXT = 2048


def _permute_cols(t, perm):
    n = len(perm)
    onehot = np.zeros((n, n), np.float32)
    onehot[perm, np.arange(n)] = 1.0
    return jnp.einsum("...k,kj->...j", t, onehot, precision=lax.Precision.HIGHEST)


def _dil_table(bias, dil):
    qm, km, nres, stride = _DIL_GEO[dil]
    heads = bias.shape[0]
    if dil == 16:
        band = jnp.stack([bias, jnp.full_like(bias, NEG)], axis=-1).reshape(heads, -1)[:, :4 * RADIUS + 1]
    else:
        band = bias
    half = band.shape[-1] // 2
    neg = jnp.full((heads, _EXT - half), NEG, F32)
    ext = jnp.concatenate([neg, band, neg], axis=-1)
    nq, nk = qm * stride, km * stride
    halo = (km - qm) // 2
    tabs = []
    for off in (-halo, 0, -2 * halo):
        lo = stride * off - (nq - 1) + _EXT
        tabs.append(_toeplitz(ext[:, lo:lo + nq + nk - 1], nq))
    t = jnp.stack(tabs, axis=1)
    if nres > 1:
        t = jnp.transpose(t.reshape(heads, 3, qm, nres, nk), (0, 1, 3, 2, 4)).reshape(heads, 3, nq, nk)
        final = np.arange(nk)
        t = _permute_cols(t, (final % km) * nres + final // km)
    return t


def _dil_tables(t5_table):
    out = []
    for _, dil in DIL_PAIRS:
        bias = t5_table[_t5_bucket(dil * np.arange(-RADIUS, RADIUS + 1))].astype(F32)
        out.append(_dil_table(bias.T * LOG2E, dil))
    return out


def _na_table(rpb, rows):
    npairs = rows // 2
    heads = rpb.shape[-1]
    sub = GRID_W // RES
    n_dr = 2 * NA_ROWS - 1
    nk = NA_WIN_ROWS * GRID_W
    c = np.arange(GRID_W)
    cs = np.clip(c - NA_COLS // 2, 0, GRID_W - NA_COLS)
    col_ok = (c[None, :] >= cs[:, None]) & (c[None, :] < cs[:, None] + NA_COLS)
    pad = GRID_W - NA_COLS
    e = jnp.pad(jnp.transpose(rpb, (2, 0, 1)).astype(F32) * LOG2E, ((0, 0), (0, 0), (pad, pad)))
    cmat = _toeplitz(e, GRID_W)
    cflat = jnp.transpose(cmat, (0, 2, 1, 3)).reshape(heads, GRID_W, n_dr * GRID_W)
    kf = np.arange(nk)
    k_rho, k_rest = np.divmod(kf, NA_WIN_M)
    k_row, k_ck = k_rest // sub, (k_rest % sub) * RES + k_rho
    src = np.zeros((5, 2, nk), np.int32)
    ok = np.zeros((5, 2, GRID_W, nk), bool)
    for v, r2 in enumerate((2, 0, 1, npairs - 2, npairs - 1)):
        w0 = int(np.clip(2 * r2 - 4, 0, rows - NA_WIN_ROWS))
        for a in range(2):
            rq = 2 * r2 + a
            rs = int(np.clip(rq - NA_ROWS // 2, 0, rows - NA_ROWS))
            rk = w0 + k_row
            dr = np.clip(rk - rq + NA_ROWS - 1, 0, n_dr - 1)
            src[v, a] = dr * GRID_W + k_ck
            ok[v, a] = ((rk >= rs) & (rk < rs + NA_ROWS))[None, :] & col_ok[:, k_ck]
    select = (jnp.arange(n_dr * GRID_W, dtype=jnp.int32)[None, None, :, None] == src[:, :, None, :])
    t = jnp.einsum("hqx,vaxk->hvaqk", cflat, select.astype(F32), precision=lax.Precision.HIGHEST)
    t = jnp.where(ok, t, NEG).reshape(heads, 5, 2 * GRID_W, nk)
    qf = np.arange(RES * NA_SUB_M)
    q_rho, q_rest = np.divmod(qf, NA_SUB_M)
    perm = (q_rest // sub) * GRID_W + (q_rest % sub) * RES + q_rho
    onehot = np.zeros((len(qf), 2 * GRID_W), np.float32)
    onehot[qf, perm] = 1.0
    return jnp.einsum("pq,hvqk->hvpk", onehot, t, precision=lax.Precision.HIGHEST)


def _na_kernel(q_ref, k_ref, v_ref, t_ref, o_ref, *, rows):
    npairs = rows // 2
    subs = TILE_M // NA_SUB_M
    tile = pl.program_id(2)

    def window(j):
        r2 = tile * subs + j
        w0 = jnp.clip(2 * r2 - 4, 0, rows - NA_WIN_ROWS)
        var = jnp.where(r2 == 0, 1, jnp.where(r2 == 1, 2, jnp.where(
            r2 == npairs - 2, 3, jnp.where(r2 == npairs - 1, 4, 0))))
        return pl.multiple_of(w0 * (GRID_W // RES), 16), var

    def scores(j):
        ws, var = window(j)
        q = jnp.concatenate([q_ref[r, pl.ds(j * NA_SUB_M, NA_SUB_M), :] for r in range(RES)], axis=0)
        k = jnp.concatenate([k_ref[r, pl.ds(ws, NA_WIN_M), :] for r in range(RES)], axis=0)
        return _dot_nt(q, k) + t_ref[var]

    pending = [scores(j) for j in range(NA_AHEAD)]
    for j in range(subs):
        if j + NA_AHEAD < subs:
            pending.append(scores(j + NA_AHEAD))
        s = pending.pop(0)
        ws, _ = window(j)
        v = jnp.concatenate([v_ref[r, pl.ds(ws, NA_WIN_M), :] for r in range(RES)], axis=0)
        m = jnp.max(s, axis=-1, keepdims=True)
        p = jnp.exp2(s - m)
        l = jnp.sum(p, axis=-1, keepdims=True)
        o = jnp.dot(p.astype(BF16), v, preferred_element_type=F32) / l
        for r in range(RES):
            o_ref[r, pl.ds(j * NA_SUB_M, NA_SUB_M), :] = o[r * NA_SUB_M:(r + 1) * NA_SUB_M].astype(BF16)


def _na_call(qkv, table, rows):
    _, B, _, M, _ = qkv.shape
    blk_q = (None, None, RES, TILE_M, HEAD_DIM)
    blk_kv = (None, None, RES, M, HEAD_DIM)
    return pl.pallas_call(
        functools.partial(_na_kernel, rows=rows),
        grid=(B, H_NA, M // TILE_M),
        in_specs=[
            pl.BlockSpec(blk_q, lambda b, h, i: (h, b, 0, i, 0)),
            pl.BlockSpec(blk_kv, lambda b, h, i: (H_NA + h, b, 0, 0, 0)),
            pl.BlockSpec(blk_kv, lambda b, h, i: (2 * H_NA + h, b, 0, 0, 0)),
            pl.BlockSpec((None,) + table.shape[1:], lambda b, h, i: (h, 0, 0, 0)),
        ],
        out_specs=pl.BlockSpec(blk_q, lambda b, h, i: (h, b, 0, i, 0)),
        out_shape=jax.ShapeDtypeStruct((H_NA, B, RES, M, HEAD_DIM), BF16),
        compiler_params=pltpu.CompilerParams(
            dimension_semantics=("parallel", "parallel", "arbitrary"), vmem_limit_bytes=VMEM_LIMIT),
        name="na_attn",
    )(qkv, qkv, qkv, table)


def _edge_variant(blk, nblk):
    return jnp.where(blk == 0, 1, jnp.where(blk == nblk - 1, 2, 0))


def _flash_step(s, v, old):
    tiles = [s[:, c:c + HEAD_DIM] for c in range(0, s.shape[1], HEAD_DIM)]
    m_new = functools.reduce(jnp.maximum, tiles)
    m_new = jnp.broadcast_to(jnp.max(m_new, axis=-1, keepdims=True), m_new.shape)
    if old is not None:
        m_old, l_old, acc_old = old
        m_new = jnp.maximum(m_old, m_new)
        alpha = jnp.exp2(m_old - m_new)
    p = [jnp.exp2(t - m_new) for t in tiles]
    l_new = functools.reduce(jnp.add, p)
    acc = jnp.dot(jnp.concatenate(p, axis=1).astype(BF16), v, preferred_element_type=F32)
    if old is not None:
        l_new = alpha * l_old + l_new
        acc = alpha * acc_old + acc
    return m_new, l_new, acc


def _rows(ref, rr, start, size):
    return jnp.concatenate([ref[r, pl.ds(start, size), :] for r in rr], axis=0)


def _dil_kernel(q_ref, k_ref, v_ref, t1_ref, t4_ref, t16_ref, o_ref, m_scr, l_scr, acc_scr, *, m_total):
    tile = pl.program_id(2)
    stats = (m_scr, l_scr, acc_scr)

    blocks = []
    for dil, t_ref in ((16, t16_ref), (4, t4_ref), (1, t1_ref)):
        qm, km, nres, _ = _DIL_GEO[dil]
        for sub in range(TILE_M // qm):
            for r0 in range(RES // nres):
                blocks.append((t_ref, qm, km, sub, tuple(range(r0, RES, RES // nres)), dil == 16))

    def window(blk):
        _, qm, km, sub, _, _ = blk
        idx = tile * (TILE_M // qm) + sub
        ws = pl.multiple_of(jnp.clip(idx * qm - (km - qm) // 2, 0, m_total - km), 16)
        return ws, _edge_variant(idx, m_total // qm)

    def scores(blk):
        t_ref, qm, km, sub, rr, _ = blk
        ws, var = window(blk)
        return _dot_nt(_rows(q_ref, rr, sub * qm, qm), _rows(k_ref, rr, ws, km)) + t_ref[var]

    def finish(blk, s):
        _, qm, km, sub, rr, first = blk
        ws, _ = window(blk)
        old = None if first else tuple(_rows(ref, rr, sub * qm, qm) for ref in stats)
        new = _flash_step(s, _rows(v_ref, rr, ws, km), old)
        for ref, val in zip(stats, new):
            for a, r in enumerate(rr):
                ref[r, pl.ds(sub * qm, qm), :] = val[a * qm:(a + 1) * qm]

    pending = [scores(blk) for blk in blocks[:DIL_AHEAD]]
    for n, blk in enumerate(blocks):
        if n + DIL_AHEAD < len(blocks):
            pending.append(scores(blocks[n + DIL_AHEAD]))
        finish(blk, pending.pop(0))

    o_ref[...] = (acc_scr[...] / jnp.sum(l_scr[...], axis=-1, keepdims=True)).astype(BF16)


def _dil_call(qkv, t1, t4, t16):
    _, B, _, M, _ = qkv.shape
    base = 3 * H_NA
    blk_q = (None, None, RES, TILE_M, HEAD_DIM)
    blk_kv = (None, None, RES, M, HEAD_DIM)
    tspec = lambda t: pl.BlockSpec((None,) + t.shape[1:], lambda b, h, i: (h, 0, 0, 0))
    stat = pltpu.VMEM((RES, TILE_M, HEAD_DIM), F32)
    return pl.pallas_call(
        functools.partial(_dil_kernel, m_total=M),
        grid=(B, H_DIL, M // TILE_M),
        in_specs=[
            pl.BlockSpec(blk_q, lambda b, h, i: (base + h, b, 0, i, 0)),
            pl.BlockSpec(blk_kv, lambda b, h, i: (base + H_DIL + h, b, 0, 0, 0)),
            pl.BlockSpec(blk_kv, lambda b, h, i: (base + 2 * H_DIL + h, b, 0, 0, 0)),
            tspec(t1), tspec(t4), tspec(t16),
        ],
        out_specs=pl.BlockSpec(blk_q, lambda b, h, i: (h, b, 0, i, 0)),
        out_shape=jax.ShapeDtypeStruct((H_DIL, B, RES, M, HEAD_DIM), BF16),
        scratch_shapes=[stat, stat, stat],
        compiler_params=pltpu.CompilerParams(
            dimension_semantics=("parallel", "parallel", "arbitrary"), vmem_limit_bytes=VMEM_LIMIT),
        name="dil_attn",
    )(qkv, qkv, qkv, t1, t4, t16)


def _trunk(x, w_in, w_out, g_attn, g_na, g_dil, na_tables, dil_tables, g_ffn, w_gate, w_up, w_down,
           g_final, q_scale):
    B, S, D = x.shape
    M = S // RES
    R = B * S
    depth = w_in.shape[0]
    assert S % (RES * TILE_M) == 0 and M >= 512 and S % (2 * GRID_W) == 0
    rows = S // GRID_W
    x8 = jnp.transpose(x.reshape(B, M, RES, D), (0, 2, 1, 3)).reshape(R, D)
    for l in range(depth):
        qkv = _qkv_call(x8, g_attn[l][None], w_in, q_scale, layer=l, tm=QKV_TM, tn=QKV_TN)
        qkv = qkv.reshape(qkv.shape[0], B, RES, M, HEAD_DIM)
        oa = _na_call(qkv, na_tables[(l, rows)], rows).reshape(H_NA, R, HEAD_DIM)
        ob = _dil_call(qkv, *dil_tables).reshape(H_DIL, R, HEAD_DIM)
        x1, h = _outproj_call(oa, ob, g_na[l][None], g_dil[l][None], w_out, x8, g_ffn[l][None],
                              layer=l, tm=OUTPROJ_TM)
        act = _ffn_up_call(h, w_gate, w_up, layer=l, tm=FFN_UP_TM, tf=FFN_UP_TF)
        x8 = _ffn_down_call(act, w_down, x1, g_final[None], layer=l, tm=FFN_DOWN_TM,
                            final_norm=(l == depth - 1))
    return jnp.transpose(x8.reshape(B, RES, M, D), (0, 2, 1, 3)).reshape(B, S, D)


def kernel(x_prompt, x_sample, w_in, w_out, g_attn, g_na, g_dil, rpb_na, t5_table, g_ffn, w_gate, w_up,
           w_down, g_final):
    depth = w_in.shape[0]
    w_na = H_NA * HEAD_DIM
    w_dil = H_DIL * HEAD_DIM
    col = np.arange(3 * (w_na + w_dil))
    is_q = (col < w_na) | ((col >= 3 * w_na) & (col < 3 * w_na + w_dil))
    q_scale = jnp.asarray(np.where(is_q, SCALE * LOG2E, 1.0)[None], F32)
    dil_tables = _dil_tables(t5_table)
    na_tables = {}
    for x in (x_prompt, x_sample):
        rows = x.shape[1] // GRID_W
        for l in range(depth):
            if (l, rows) not in na_tables:
                na_tables[(l, rows)] = _na_table(rpb_na[l], rows)
    wb = [w.astype(BF16) for w in (w_in, w_out, w_gate, w_up, w_down)]
    outs = []
    for x in (x_prompt, x_sample):
        outs.append(_trunk(x, wb[0], wb[1], g_attn, g_na, g_dil, na_tables, dil_tables, g_ffn,
                           wb[2], wb[3], wb[4], g_final, q_scale))
    return tuple(outs)
```

```python
import functools
import math

import jax
import jax.numpy as jnp
import numpy as np
from jax import lax
from jax.experimental import pallas as pl
from jax.experimental.pallas import tpu as pltpu

HEAD_DIM = 128
H_NA = 8
H_DIL = 8
GRID_W = 64
NA_ROWS = 8
NA_COLS = 16
DIL_PAIRS = ((128, 1), (512, 4), (2048, 16))
T5_BUCKETS = 32
T5_MAX_DIST = 2048
EPS = 1e-6
NEG = -1e30
SCALE = 1.0 / math.sqrt(HEAD_DIM)
LOG2E = math.log2(math.e)

RES = 8
RADIUS = 64
TILE_M = 512
NA_SUB_M = 16
NA_WIN_ROWS = 10
NA_WIN_M = NA_WIN_ROWS * GRID_W // RES
DIL_AHEAD = 5
NA_AHEAD = 3
QKV_TM, QKV_TN = 1024, 2048
QKV_CHUNKS = 4
OUTPROJ_TM = 512
FFN_UP_TM, FFN_UP_TF = 4096, 512
FFN_UP_CHUNKS = 16
FFN_DOWN_TM = 512
VMEM_LIMIT = 56 * 1024 * 1024

BF16 = jnp.bfloat16
F32 = jnp.float32


def _dot_nt(a, b):
    return lax.dot_general(a, b, (((1,), (1,)), ((), ())), preferred_element_type=F32)


def _rmsnorm_rows(x, g):
    ms = jnp.mean(x * x, axis=-1, keepdims=True)
    return x * lax.rsqrt(ms + EPS) * g


def _qkv_kernel(x_ref, g_ref, w_ref, sc_ref, o_ref, h_scr):
    rows = x_ref.shape[0] // QKV_CHUNKS

    def project(k):
        r = pl.ds(k * rows, rows)
        acc = jnp.dot(h_scr[r, :], w_ref[...], preferred_element_type=F32) * sc_ref[...]
        for c in range(o_ref.shape[0]):
            o_ref[c, r, :] = acc[:, c * HEAD_DIM:(c + 1) * HEAD_DIM].astype(BF16)

    @pl.when(pl.program_id(1) == 0)
    def _():
        for k in range(QKV_CHUNKS):
            r = pl.ds(k * rows, rows)
            h_scr[r, :] = _rmsnorm_rows(x_ref[r, :], g_ref[...]).astype(BF16)
            project(k)

    @pl.when(pl.program_id(1) != 0)
    def _():
        for k in range(QKV_CHUNKS):
            project(k)


def _qkv_call(x, g, w, sc, layer, tm, tn):
    R, D = x.shape
    N = w.shape[2]
    return pl.pallas_call(
        _qkv_kernel,
        grid=(R // tm, N // tn),
        in_specs=[
            pl.BlockSpec((tm, D), lambda i, j: (i, 0)),
            pl.BlockSpec((1, D), lambda i, j: (0, 0)),
            pl.BlockSpec((None, D, tn), lambda i, j: (layer, 0, j)),
            pl.BlockSpec((1, tn), lambda i, j: (0, j)),
        ],
        out_specs=pl.BlockSpec((tn // HEAD_DIM, tm, HEAD_DIM), lambda i, j: (j, i, 0)),
        out_shape=jax.ShapeDtypeStruct((N // HEAD_DIM, R, HEAD_DIM), BF16),
        scratch_shapes=[pltpu.VMEM((tm, D), BF16)],
        compiler_params=pltpu.CompilerParams(
            dimension_semantics=("parallel", "arbitrary"), vmem_limit_bytes=VMEM_LIMIT),
        name="qkv_proj",
    )(x, g, w, sc)


def _group_norm(ref, g):
    a = jnp.concatenate([ref[c] for c in range(ref.shape[0])], axis=1).astype(F32)
    return _rmsnorm_rows(a, g).astype(BF16)


def _outproj_kernel(oa_ref, ob_ref, ga_ref, gb_ref, wo_ref, x_ref, g_ref, x1_ref, h_ref):
    mix = jnp.concatenate([_group_norm(oa_ref, ga_ref[...]), _group_norm(ob_ref, gb_ref[...])], axis=1)
    x = x_ref[...] + jnp.dot(mix, wo_ref[...], preferred_element_type=F32)
    x1_ref[...] = x
    h_ref[...] = _rmsnorm_rows(x, g_ref[...]).astype(BF16)


def _outproj_call(oa, ob, ga, gb, wo, x, g, layer, tm):
    R, D = x.shape
    ha, hb = oa.shape[0], ob.shape[0]
    row = lambda i: (0, 0)
    return pl.pallas_call(
        _outproj_kernel,
        grid=(R // tm,),
        in_specs=[
            pl.BlockSpec((ha, tm, HEAD_DIM), lambda i: (0, i, 0)),
            pl.BlockSpec((hb, tm, HEAD_DIM), lambda i: (0, i, 0)),
            pl.BlockSpec((1, ha * HEAD_DIM), row),
            pl.BlockSpec((1, hb * HEAD_DIM), row),
            pl.BlockSpec((None,) + wo.shape[1:], lambda i: (layer, 0, 0), pipeline_mode=pl.Buffered(1)),
            pl.BlockSpec((tm, D), lambda i: (i, 0)),
            pl.BlockSpec((1, D), row),
        ],
        out_specs=[pl.BlockSpec((tm, D), lambda i: (i, 0)), pl.BlockSpec((tm, D), lambda i: (i, 0))],
        out_shape=[jax.ShapeDtypeStruct((R, D), F32), jax.ShapeDtypeStruct((R, D), BF16)],
        compiler_params=pltpu.CompilerParams(
            dimension_semantics=("parallel",), vmem_limit_bytes=VMEM_LIMIT),
        name="out_proj",
    )(oa, ob, ga, gb, wo, x, g)


def _ffn_up_kernel(h_ref, wg_ref, wu_ref, a_ref):
    rows = h_ref.shape[0] // FFN_UP_CHUNKS
    for k in range(FFN_UP_CHUNKS):
        h = h_ref[k * rows:(k + 1) * rows]
        gate = jnp.dot(h, wg_ref[...], preferred_element_type=F32)
        up = jnp.dot(h, wu_ref[...], preferred_element_type=F32)
        a_ref[k * rows:(k + 1) * rows] = (gate * (1.0 / (1.0 + jnp.exp(-gate))) * up).astype(BF16)


def _ffn_up_call(h, wg, wu, layer, tm, tf):
    R, D = h.shape
    F = wg.shape[2]
    return pl.pallas_call(
        _ffn_up_kernel,
        grid=(R // tm, F // tf),
        in_specs=[
            pl.BlockSpec((tm, D), lambda i, f: (i, 0)),
            pl.BlockSpec((None, D, tf), lambda i, f: (layer, 0, f)),
            pl.BlockSpec((None, D, tf), lambda i, f: (layer, 0, f)),
        ],
        out_specs=pl.BlockSpec((tm, tf), lambda i, f: (i, f)),
        out_shape=jax.ShapeDtypeStruct((R, F), BF16),
        compiler_params=pltpu.CompilerParams(
            dimension_semantics=("parallel", "arbitrary"), vmem_limit_bytes=VMEM_LIMIT),
        name="ffn_up",
    )(h, wg, wu)


def _ffn_down_kernel(a_ref, wd_ref, x_ref, gf_ref, o_ref, *, final_norm):
    y = x_ref[...] + jnp.dot(a_ref[...], wd_ref[...], preferred_element_type=F32)
    if final_norm:
        y = _rmsnorm_rows(y, gf_ref[...])
    o_ref[...] = y


def _ffn_down_call(act, wd, x1, gf, layer, tm, final_norm):
    R, F = act.shape
    D = x1.shape[1]
    return pl.pallas_call(
        functools.partial(_ffn_down_kernel, final_norm=final_norm),
        grid=(R // tm,),
        in_specs=[
            pl.BlockSpec((tm, F), lambda i: (i, 0)),
            pl.BlockSpec((None, F, D), lambda i: (layer, 0, 0), pipeline_mode=pl.Buffered(1)),
            pl.BlockSpec((tm, D), lambda i: (i, 0)),
            pl.BlockSpec((1, D), lambda i: (0, 0)),
        ],
        out_specs=pl.BlockSpec((tm, D), lambda i: (i, 0)),
        out_shape=jax.ShapeDtypeStruct((R, D), F32),
        compiler_params=pltpu.CompilerParams(
            dimension_semantics=("parallel",), vmem_limit_bytes=VMEM_LIMIT),
        name="ffn_down",
    )(act, wd, x1, gf)


def _t5_bucket(rel):
    half = T5_BUCKETS // 2
    max_exact = half // 2
    n = np.abs(rel)
    large = max_exact + (np.log(np.maximum(n, max_exact) / max_exact)
                         / np.log(T5_MAX_DIST / max_exact) * (half - max_exact)).astype(np.int32)
    large = np.minimum(large, half - 1)
    return (rel > 0).astype(np.int32) * half + np.where(n < max_exact, n, large).astype(np.int32)


def _toeplitz(e, nq):
    length = e.shape[-1]
    f = jnp.concatenate([e, jnp.zeros(e.shape[:-1] + (1,), e.dtype)], axis=-1)
    flat = jnp.tile(f, (1,) * (e.ndim - 1) + (nq,))[..., :nq * length]
    return flat.reshape(e.shape[:-1] + (nq, length))[..., nq - 1:]


_DIL_GEO = {16: (128, 384, 1, 1), 4: (64, 128, 2, 2), 1: (32, 64, RES, RES)}
_EXT = 2048


def _permute_cols(t, perm):
    n = len(perm)
    onehot = np.zeros((n, n), np.float32)
    onehot[perm, np.arange(n)] = 1.0
    return jnp.einsum("...k,kj->...j", t, onehot, precision=lax.Precision.HIGHEST)


def _dil_table(bias, dil):
    qm, km, nres, stride = _DIL_GEO[dil]
    heads = bias.shape[0]
    if dil == 16:
        band = jnp.stack([bias, jnp.full_like(bias, NEG)], axis=-1).reshape(heads, -1)[:, :4 * RADIUS + 1]
    else:
        band = bias
    half = band.shape[-1] // 2
    neg = jnp.full((heads, _EXT - half), NEG, F32)
    ext = jnp.concatenate([neg, band, neg], axis=-1)
    nq, nk = qm * stride, km * stride
    halo = (km - qm) // 2
    tabs = []
    if nres == RES:
        span = qm + km - 1
        for off in (-halo, 0, -2 * halo):
            lo = stride * (off - (qm - 1)) + _EXT
            vecs = jnp.stack([ext[:, lo + d:lo + d + stride * span:stride] for d in range(1 - RES, RES)], axis=1)
            blocks = jnp.transpose(_toeplitz(vecs, qm), (0, 2, 1, 3)).reshape(heads, qm, (2 * RES - 1) * km)
            rows = [blocks[:, :, (RES - 1 - r) * km:(2 * RES - 1 - r) * km] for r in range(RES)]
            tabs.append(jnp.stack(rows, axis=1).reshape(heads, nq, nk))
        return jnp.stack(tabs, axis=1)
    for off in (-halo, 0, -2 * halo):
        lo = stride * off - (nq - 1) + _EXT
        tabs.append(_toeplitz(ext[:, lo:lo + nq + nk - 1], nq))
    t = jnp.stack(tabs, axis=1)
    if nres > 1:
        t = jnp.transpose(t.reshape(heads, 3, qm, nres, nk), (0, 1, 3, 2, 4)).reshape(heads, 3, nq, nk)
        final = np.arange(nk)
        t = _permute_cols(t, (final % km) * nres + final // km)
    return t


def _dil_tables(t5_table):
    out = []
    for _, dil in DIL_PAIRS:
        bias = t5_table[_t5_bucket(dil * np.arange(-RADIUS, RADIUS + 1))].astype(F32)
        out.append(_dil_table(bias.T * LOG2E, dil))
    return out


def _na_table(rpb, rows):
    npairs = rows // 2
    heads = rpb.shape[-1]
    sub = GRID_W // RES
    n_dr = 2 * NA_ROWS - 1
    nk = NA_WIN_ROWS * GRID_W
    c = np.arange(GRID_W)
    cs = np.clip(c - NA_COLS // 2, 0, GRID_W - NA_COLS)
    col_ok = (c[None, :] >= cs[:, None]) & (c[None, :] < cs[:, None] + NA_COLS)
    pad = GRID_W - NA_COLS
    e = jnp.pad(jnp.transpose(rpb, (2, 0, 1)).astype(F32) * LOG2E, ((0, 0), (0, 0), (pad, pad)))
    cmat = _toeplitz(e, GRID_W)
    cflat = jnp.transpose(cmat, (0, 2, 1, 3)).reshape(heads, GRID_W, n_dr * GRID_W)
    kf = np.arange(nk)
    k_rho, k_rest = np.divmod(kf, NA_WIN_M)
    k_row, k_ck = k_rest // sub, (k_rest % sub) * RES + k_rho
    src = np.zeros((5, 2, nk), np.int32)
    ok = np.zeros((5, 2, GRID_W, nk), bool)
    for v, r2 in enumerate((2, 0, 1, npairs - 2, npairs - 1)):
        w0 = int(np.clip(2 * r2 - 4, 0, rows - NA_WIN_ROWS))
        for a in range(2):
            rq = 2 * r2 + a
            rs = int(np.clip(rq - NA_ROWS // 2, 0, rows - NA_ROWS))
            rk = w0 + k_row
            dr = np.clip(rk - rq + NA_ROWS - 1, 0, n_dr - 1)
            src[v, a] = dr * GRID_W + k_ck
            ok[v, a] = ((rk >= rs) & (rk < rs + NA_ROWS))[None, :] & col_ok[:, k_ck]
    select = (jnp.arange(n_dr * GRID_W, dtype=jnp.int32)[None, None, :, None] == src[:, :, None, :])
    t = jnp.einsum("hqx,vaxk->hvaqk", cflat, select.astype(F32), precision=lax.Precision.HIGHEST)
    t = jnp.where(ok, t, NEG).reshape(heads, 5, 2 * GRID_W, nk)
    qf = np.arange(RES * NA_SUB_M)
    q_rho, q_rest = np.divmod(qf, NA_SUB_M)
    perm = (q_rest // sub) * GRID_W + (q_rest % sub) * RES + q_rho
    onehot = np.zeros((len(qf), 2 * GRID_W), np.float32)
    onehot[qf, perm] = 1.0
    return jnp.einsum("pq,hvqk->hvpk", onehot, t, precision=lax.Precision.HIGHEST)


def _na_kernel(q_ref, k_ref, v_ref, t_ref, o_ref, *, rows):
    npairs = rows // 2
    subs = TILE_M // NA_SUB_M
    tile = pl.program_id(2)

    def window(j):
        r2 = tile * subs + j
        w0 = jnp.clip(2 * r2 - 4, 0, rows - NA_WIN_ROWS)
        var = jnp.where(r2 == 0, 1, jnp.where(r2 == 1, 2, jnp.where(
            r2 == npairs - 2, 3, jnp.where(r2 == npairs - 1, 4, 0))))
        return pl.multiple_of(w0 * (GRID_W // RES), 16), var

    def scores(j):
        ws, var = window(j)
        q = jnp.concatenate([q_ref[r, pl.ds(j * NA_SUB_M, NA_SUB_M), :] for r in range(RES)], axis=0)
        k = jnp.concatenate([k_ref[r, pl.ds(ws, NA_WIN_M), :] for r in range(RES)], axis=0)
        return _dot_nt(q, k) + t_ref[var]

    pending = [scores(j) for j in range(NA_AHEAD)]
    for j in range(subs):
        if j + NA_AHEAD < subs:
            pending.append(scores(j + NA_AHEAD))
        s = pending.pop(0)
        ws, _ = window(j)
        v = jnp.concatenate([v_ref[r, pl.ds(ws, NA_WIN_M), :] for r in range(RES)], axis=0)
        m = jnp.max(s, axis=-1, keepdims=True)
        p = jnp.exp2(s - m)
        l = jnp.sum(p, axis=-1, keepdims=True)
        o = jnp.dot(p.astype(BF16), v, preferred_element_type=F32) / l
        for r in range(RES):
            o_ref[r, pl.ds(j * NA_SUB_M, NA_SUB_M), :] = o[r * NA_SUB_M:(r + 1) * NA_SUB_M].astype(BF16)


def _na_call(qkv, table, rows):
    _, B, _, M, _ = qkv.shape
    blk_q = (None, None, RES, TILE_M, HEAD_DIM)
    blk_kv = (None, None, RES, M, HEAD_DIM)
    return pl.pallas_call(
        functools.partial(_na_kernel, rows=rows),
        grid=(B, H_NA, M // TILE_M),
        in_specs=[
            pl.BlockSpec(blk_q, lambda b, h, i: (h, b, 0, i, 0)),
            pl.BlockSpec(blk_kv, lambda b, h, i: (H_NA + h, b, 0, 0, 0)),
            pl.BlockSpec(blk_kv, lambda b, h, i: (2 * H_NA + h, b, 0, 0, 0)),
            pl.BlockSpec((None,) + table.shape[1:], lambda b, h, i: (h, 0, 0, 0)),
        ],
        out_specs=pl.BlockSpec(blk_q, lambda b, h, i: (h, b, 0, i, 0)),
        out_shape=jax.ShapeDtypeStruct((H_NA, B, RES, M, HEAD_DIM), BF16),
        compiler_params=pltpu.CompilerParams(
            dimension_semantics=("parallel", "parallel", "arbitrary"), vmem_limit_bytes=VMEM_LIMIT),
        name="na_attn",
    )(qkv, qkv, qkv, table)


def _edge_variant(blk, nblk):
    return jnp.where(blk == 0, 1, jnp.where(blk == nblk - 1, 2, 0))


def _flash_step(s, v, old):
    tiles = [s[:, c:c + HEAD_DIM] for c in range(0, s.shape[1], HEAD_DIM)]
    m_new = functools.reduce(jnp.maximum, tiles)
    m_new = jnp.broadcast_to(jnp.max(m_new, axis=-1, keepdims=True), m_new.shape)
    if old is not None:
        m_old, l_old, acc_old = old
        m_new = jnp.maximum(m_old, m_new)
        alpha = jnp.exp2(m_old - m_new)
    p = [jnp.exp2(t - m_new) for t in tiles]
    l_new = functools.reduce(jnp.add, p)
    acc = jnp.dot(jnp.concatenate(p, axis=1).astype(BF16), v, preferred_element_type=F32)
    if old is not None:
        l_new = alpha * l_old + l_new
        acc = alpha * acc_old + acc
    return m_new, l_new, acc


def _rows(ref, rr, start, size):
    return jnp.concatenate([ref[r, pl.ds(start, size), :] for r in rr], axis=0)


def _dil_kernel(q_ref, k_ref, v_ref, t1_ref, t4_ref, t16_ref, o_ref, m_scr, l_scr, acc_scr, *, m_total):
    tile = pl.program_id(2)
    stats = (m_scr, l_scr, acc_scr)

    blocks = []
    for dil, t_ref in ((16, t16_ref), (4, t4_ref), (1, t1_ref)):
        qm, km, nres, _ = _DIL_GEO[dil]
        for sub in range(TILE_M // qm):
            for r0 in range(RES // nres):
                blocks.append((t_ref, qm, km, sub, tuple(range(r0, RES, RES // nres)), dil == 16))

    def window(blk):
        _, qm, km, sub, _, _ = blk
        idx = tile * (TILE_M // qm) + sub
        ws = pl.multiple_of(jnp.clip(idx * qm - (km - qm) // 2, 0, m_total - km), 16)
        return ws, _edge_variant(idx, m_total // qm)

    def scores(blk):
        t_ref, qm, km, sub, rr, _ = blk
        ws, var = window(blk)
        return _dot_nt(_rows(q_ref, rr, sub * qm, qm), _rows(k_ref, rr, ws, km)) + t_ref[var]

    def finish(blk, s):
        _, qm, km, sub, rr, first = blk
        ws, _ = window(blk)
        old = None if first else tuple(_rows(ref, rr, sub * qm, qm) for ref in stats)
        new = _flash_step(s, _rows(v_ref, rr, ws, km), old)
        for ref, val in zip(stats, new):
            for a, r in enumerate(rr):
                ref[r, pl.ds(sub * qm, qm), :] = val[a * qm:(a + 1) * qm]

    pending = [scores(blk) for blk in blocks[:DIL_AHEAD]]
    for n, blk in enumerate(blocks):
        if n + DIL_AHEAD < len(blocks):
            pending.append(scores(blocks[n + DIL_AHEAD]))
        finish(blk, pending.pop(0))

    o_ref[...] = (acc_scr[...] / jnp.sum(l_scr[...], axis=-1, keepdims=True)).astype(BF16)


def _dil_call(qkv, t1, t4, t16):
    _, B, _, M, _ = qkv.shape
    base = 3 * H_NA
    blk_q = (None, None, RES, TILE_M, HEAD_DIM)
    blk_kv = (None, None, RES, M, HEAD_DIM)
    tspec = lambda t: pl.BlockSpec((None,) + t.shape[1:], lambda b, h, i: (h, 0, 0, 0))
    stat = pltpu.VMEM((RES, TILE_M, HEAD_DIM), F32)
    return pl.pallas_call(
        functools.partial(_dil_kernel, m_total=M),
        grid=(B, H_DIL, M // TILE_M),
        in_specs=[
            pl.BlockSpec(blk_q, lambda b, h, i: (base + h, b, 0, i, 0)),
            pl.BlockSpec(blk_kv, lambda b, h, i: (base + H_DIL + h, b, 0, 0, 0)),
            pl.BlockSpec(blk_kv, lambda b, h, i: (base + 2 * H_DIL + h, b, 0, 0, 0)),
            tspec(t1), tspec(t4), tspec(t16),
        ],
        out_specs=pl.BlockSpec(blk_q, lambda b, h, i: (h, b, 0, i, 0)),
        out_shape=jax.ShapeDtypeStruct((H_DIL, B, RES, M, HEAD_DIM), BF16),
        scratch_shapes=[stat, stat, stat],
        compiler_params=pltpu.CompilerParams(
            dimension_semantics=("parallel", "parallel", "arbitrary"), vmem_limit_bytes=VMEM_LIMIT),
        name="dil_attn",
    )(qkv, qkv, qkv, t1, t4, t16)


def _trunk(x, w_in, w_out, g_attn, g_na, g_dil, na_tables, dil_tables, g_ffn, w_gate, w_up, w_down,
           g_final, q_scale):
    B, S, D = x.shape
    M = S // RES
    R = B * S
    depth = w_in.shape[0]
    assert S % (RES * TILE_M) == 0 and M >= 512 and S % (2 * GRID_W) == 0
    rows = S // GRID_W
    x8 = jnp.transpose(x.reshape(B, M, RES, D), (0, 2, 1, 3)).reshape(R, D)
    for l in range(depth):
        qkv = _qkv_call(x8, g_attn[l][None], w_in, q_scale, layer=l, tm=QKV_TM, tn=QKV_TN)
        qkv = qkv.reshape(qkv.shape[0], B, RES, M, HEAD_DIM)
        oa = _na_call(qkv, na_tables[(l, rows)], rows).reshape(H_NA, R, HEAD_DIM)
        ob = _dil_call(qkv, *dil_tables).reshape(H_DIL, R, HEAD_DIM)
        x1, h = _outproj_call(oa, ob, g_na[l][None], g_dil[l][None], w_out, x8, g_ffn[l][None],
                              layer=l, tm=OUTPROJ_TM)
        act = _ffn_up_call(h, w_gate, w_up, layer=l, tm=FFN_UP_TM, tf=FFN_UP_TF)
        x8 = _ffn_down_call(act, w_down, x1, g_final[None], layer=l, tm=FFN_DOWN_TM,
                            final_norm=(l == depth - 1))
    return jnp.transpose(x8.reshape(B, RES, M, D), (0, 2, 1, 3)).reshape(B, S, D)


def kernel(x_prompt, x_sample, w_in, w_out, g_attn, g_na, g_dil, rpb_na, t5_table, g_ffn, w_gate, w_up,
           w_down, g_final):
    depth = w_in.shape[0]
    w_na = H_NA * HEAD_DIM
    w_dil = H_DIL * HEAD_DIM
    col = np.arange(3 * (w_na + w_dil))
    is_q = (col < w_na) | ((col >= 3 * w_na) & (col < 3 * w_na + w_dil))
    q_scale = jnp.asarray(np.where(is_q, SCALE * LOG2E, 1.0)[None], F32)
    dil_tables = _dil_tables(t5_table)
    na_tables = {}
    for x in (x_prompt, x_sample):
        rows = x.shape[1] // GRID_W
        for l in range(depth):
            if (l, rows) not in na_tables:
                na_tables[(l, rows)] = _na_table(rpb_na[l], rows)
    wb = [w.astype(BF16) for w in (w_in, w_out, w_gate, w_up, w_down)]
    outs = []
    for x in (x_prompt, x_sample):
        outs.append(_trunk(x, wb[0], wb[1], g_attn, g_na, g_dil, na_tables, dil_tables, g_ffn,
                           wb[2], wb[3], wb[4], g_final, q_scale))
    return tuple(outs)
```
